```python
import math
import jax
import jax.numpy as jnp
from jax import lax
import numpy as np

D_MODEL = 1024
BATCH = 4
SEQ = 4096
DEPTH = 4

GRID_W = 64
CTX_LEN = 256

GLA_HEADS = 4
GLA_DK = 32
GLA_DV = 64
GLA_RANK = 16
GLA_TAU = 16.0
GLA_CHUNK = 32
ROPE_BASE = 10000.0
NA_HEADS = 4
NA_HD = 64
NA_WIN_R = 8
NA_WIN_C = 16
GDN_HEADS = 4
GDN_DK = 128
GDN_DV = 128
GDN_CONV = 5
GDN_CHUNK = 64
N_EXPERTS = 32
TOP_K = 4
D_EXPERT = D_MODEL
SWIGLU_LIMIT = 7.0
SWIGLU_ALPHA = 1.702
MOE_BLOCK = 256

N_MOD = 6
LN_EPS = 1e-5
RMS_EPS = 1e-6
DN_ALPHA = (2 * DEPTH) ** 0.25
DN_BETA = (8 * DEPTH) ** -0.25

IN_SPLITS = (GLA_HEADS * GLA_DK, GLA_HEADS * GLA_DK, GLA_HEADS * GLA_DV, 2 * GLA_RANK, GLA_HEADS * GLA_DV,
             NA_HEADS * NA_HD, NA_HEADS * NA_HD, NA_HEADS * NA_HD,
             GDN_HEADS * GDN_DK, GDN_HEADS * GDN_DK, GDN_HEADS * GDN_DV, 2 * GDN_HEADS, 2 * GDN_HEADS, GDN_HEADS * GDN_DV)
D_IN = sum(IN_SPLITS)
MIX_W = GLA_HEADS * GLA_DV + NA_HEADS * NA_HD + GDN_HEADS * GDN_DV

kernel_name = 'hybrid_gla_natten_gdn_moe_trunk'

F32 = jnp.float32


def _ln(t):
    tf = t.astype(F32)
    mu = tf.mean(-1, keepdims=True)
    var = jnp.mean(jnp.square(tf - mu), -1, keepdims=True)
    return ((tf - mu) * lax.rsqrt(var + LN_EPS)).astype(t.dtype)


def _post_ln(t, g, b):
    return _ln(t) * g + b


def _modulate(t, shift, scale):
    return _ln(t) * (1 + scale) + shift


def _rmsnorm(t, g):
    tf = t.astype(F32)
    y = tf * lax.rsqrt(jnp.mean(tf * tf, -1, keepdims=True) + RMS_EPS)
    return (y * g).astype(t.dtype)


def _l2norm(t):
    tf = t.astype(F32)
    return (tf * lax.rsqrt(jnp.sum(tf * tf, -1, keepdims=True) + 1e-6)).astype(t.dtype)


def _split_cols(p):
    return jnp.split(p, np.cumsum(IN_SPLITS)[:-1].tolist(), axis=-1)


def _axial_rope_tables(n_tok):
    t = jnp.arange(n_tok)
    half = GLA_DK // 2
    inv_freq = ROPE_BASE ** (-jnp.arange(0, half, 2, dtype=F32) / half)

    def cs(pos):
        ang = pos.astype(F32)[:, None] * inv_freq
        return jnp.cos(ang)[:, None, :], jnp.sin(ang)[:, None, :]

    return cs(t // GRID_W) + cs(t % GRID_W)


def _rotate(t, cos, sin):
    t1, t2 = jnp.split(t, 2, -1)
    return jnp.concatenate([t1 * cos - t2 * sin, t2 * cos + t1 * sin], -1)


def _axial_rope(t, rot):
    cos_r, sin_r, cos_c, sin_c = rot
    tr, tc = jnp.split(t, 2, -1)
    return jnp.concatenate([_rotate(tr, cos_r, sin_r), _rotate(tc, cos_c, sin_c)], -1).astype(t.dtype)


def _blocks(t, chunk):
    b_, T, H = t.shape[:3]
    return t.astype(F32).reshape(b_, T // chunk, chunk, H, -1).transpose(1, 0, 3, 2, 4)


def _unblocks(o, dtype):
    n, b_, H, C, d = o.shape
    return o.transpose(1, 0, 3, 2, 4).reshape(b_, n * C, H, d).astype(dtype)


def _gla_chunked(q, k, v, log_a, s0):
    C = GLA_CHUNK
    qc, kc, vc = _blocks(q, C), _blocks(k, C), _blocks(v, C)
    gc = jnp.cumsum(_blocks(log_a, C), axis=3)
    tril = jnp.tril(jnp.ones((C, C), dtype=bool))
    diff = jnp.where(tril[:, :, None], gc[..., :, None, :] - gc[..., None, :, :], -jnp.inf)
    attn = jnp.einsum('nbhtd,nbhsd,nbhtsd->nbhts', qc, kc, jnp.exp(diff))
    o_intra = jnp.einsum('nbhts,nbhsv->nbhtv', attn, vc)
    q_dec = qc * jnp.exp(gc)
    k_dec = kc * jnp.exp(gc[..., -1:, :] - gc)
    a_last = jnp.exp(gc[..., -1, :])

    def step(s, inp):
        qd, kd, vv, al = inp
        o = jnp.einsum('bhtk,bhkv->bhtv', qd, s)
        s = al[..., None] * s + jnp.einsum('bhtk,bhtv->bhkv', kd, vv)
        return s, o

    s_T, o_inter = lax.scan(step, s0.astype(F32), (q_dec, k_dec, vc, a_last))
    return _unblocks(o_intra + o_inter, v.dtype), s_T


def _gdn_chunked(q, k, v, g, beta, s0):
    C = GDN_CHUNK
    dv = v.shape[-1]
    qc, kc, vc = _blocks(q, C), _blocks(k, C), _blocks(v, C)
    gc = jnp.cumsum(_blocks(g[..., None], C)[..., 0], axis=-1)
    bc = _blocks(beta[..., None], C)[..., 0]
    tril = jnp.tril(jnp.ones((C, C), dtype=bool))
    strict = jnp.tril(jnp.ones((C, C), dtype=bool), -1)
    decay = jnp.exp(jnp.where(tril, gc[..., :, None] - gc[..., None, :], -jnp.inf))
    kb = kc * bc[..., None]
    a_mat = jnp.where(strict, jnp.einsum('nbhtk,nbhsk->nbhts', kb, kc) * decay, 0.0) + jnp.eye(C, dtype=F32)
    rhs = jnp.concatenate([vc * bc[..., None], kb * jnp.exp(gc)[..., None]], axis=-1)
    sol = lax.linalg.triangular_solve(a_mat, rhs, left_side=True, lower=True)
    u, w = sol[..., :dv], sol[..., dv:]
    qk = jnp.where(tril, jnp.einsum('nbhtk,nbhsk->nbhts', qc, kc) * decay, 0.0)
    q_dec = qc * jnp.exp(gc)[..., None]
    k_dec = kc * jnp.exp(gc[..., -1:] - gc)[..., None]
    a_last = jnp.exp(gc[..., -1])

    def step(s, inp):
        qd, kd, uu, ww, qkk, al = inp
        v_new = uu - jnp.einsum('bhtk,bhkv->bhtv', ww, s)
        o = jnp.einsum('bhtk,bhkv->bhtv', qd, s) + jnp.einsum('bhts,bhsv->bhtv', qkk, v_new)
        s = al[..., None, None] * s + jnp.einsum('bhtk,bhtv->bhkv', kd, v_new)
        return s, o

    s_T, o = lax.scan(step, s0.astype(F32), (q_dec, k_dec, u, w, qk, a_last))
    return _unblocks(o, v.dtype), s_T


def _bidirectional(scan, ctx_fwd, ctx_bwd, lat_fwd, lat_bwd, s0):
    flip = lambda args: tuple(jnp.flip(a, 1) for a in args)
    o_cf, s_f = scan(*ctx_fwd, s0)
    o_cb, s_b = scan(*flip(ctx_bwd), s0)
    o_lf, _ = scan(*lat_fwd, s_f)
    o_lb, _ = scan(*flip(lat_bwd), s_b)
    return o_lf + jnp.flip(o_lb, 1), o_cf + jnp.flip(o_cb, 1)


def _gla_group(cols_l, cols_c, rope, w_a2, b_a, norm_g):
    def prep(cols, rot):
        q, k, v, a_lr, g = cols
        b_, T = q.shape[:2]
        q = q.reshape(b_, T, GLA_HEADS, GLA_DK) * GLA_DK ** -0.5
        k = k.reshape(b_, T, GLA_HEADS, GLA_DK)
        if rot is not None:
            q, k = _axial_rope(q, rot), _axial_rope(k, rot)
        v = v.reshape(b_, T, GLA_HEADS, GLA_DV)
        z = jnp.einsum('btir,irk->btik', a_lr.reshape(b_, T, 2, GLA_RANK), w_a2) + b_a
        log_a = (jax.nn.log_sigmoid(z.astype(F32)) / GLA_TAU).reshape(b_, T, 2, GLA_HEADS, GLA_DK)
        return (q, k, v, log_a[:, :, 0]), (q, k, v, log_a[:, :, 1]), g

    fl, bl, g_l = prep(cols_l, rope)
    fc, bc, g_c = prep(cols_c, None)
    s0 = jnp.zeros((g_l.shape[0], GLA_HEADS, GLA_DK, GLA_DV), F32)
    o_l, o_c = _bidirectional(_gla_chunked, fc, bc, fl, bl, s0)
    out = lambda o, g: _rmsnorm(o, norm_g).reshape(g.shape) * jax.nn.silu(g)
    return out(o_l, g_l), out(o_c, g_c)


def _neighbourhood_attn(q, k, v, k_ctx, v_ctx, rpb):
    b_, S, H, hd = q.shape
    rows = S // GRID_W
    kr = min(NA_WIN_R, rows)
    scale = hd ** -0.5
    grid = lambda t: t.reshape(b_, rows, GRID_W, H, hd)
    qg, kg, vg = grid(q), grid(k), grid(v)
    r = jnp.arange(rows)
    r0 = jnp.clip(r - kr // 2, 0, rows - kr)
    key_rows = r0[:, None] + jnp.arange(kr)
    kb = kg[:, key_rows]
    vb = vg[:, key_rows].reshape(b_, rows, kr * GRID_W, H, hd)
    s_loc = jnp.einsum('brqhd,brikhd->bhrqik', qg, kb).astype(F32) * scale
    cidx = jnp.arange(GRID_W)
    c0 = jnp.clip(cidx - NA_WIN_C // 2, 0, GRID_W - NA_WIN_C)
    col_ok = (cidx[None, :] >= c0[:, None]) & (cidx[None, :] < c0[:, None] + NA_WIN_C)
    dr = key_rows - r[:, None]
    dc = jnp.clip(cidx[None, :] - cidx[:, None] + NA_WIN_C - 1, 0, 2 * NA_WIN_C - 2)
    bias = rpb[:, (dr + NA_WIN_R - 1)[:, None, :, None], dc[None, :, None, :]].astype(F32)
    s_loc = jnp.where(col_ok[:, None, :], s_loc + bias, -jnp.inf).reshape(b_, H, rows, GRID_W, kr * GRID_W)
    s_ctx = jnp.einsum('brqhd,blhd->bhrql', qg, k_ctx).astype(F32) * scale
    p = jax.nn.softmax(jnp.concatenate([s_loc, s_ctx], -1), axis=-1)
    n_loc = kr * GRID_W
    o = (jnp.einsum('bhrqj,brjhd->brqhd', p[..., :n_loc], vb.astype(F32))
         + jnp.einsum('bhrql,blhd->brqhd', p[..., n_loc:], v_ctx.astype(F32)))
    return o.reshape(b_, S, H * hd).astype(q.dtype)


def _na_group(cols_l, cols_c, rpb):
    heads = lambda t: t.reshape(t.shape[0], t.shape[1], NA_HEADS, NA_HD)
    ql, kl, vl = (heads(t) for t in cols_l)
    qc, kc, vc = (heads(t) for t in cols_c)
    o_l = _neighbourhood_attn(ql, kl, vl, kc, vc, rpb)
    s = jnp.einsum('bqhd,bkhd->bhqk', qc, kc).astype(F32) * NA_HD ** -0.5
    o_c = jnp.einsum('bhqk,bkhd->bqhd', jax.nn.softmax(s, axis=-1), vc.astype(F32))
    return o_l, o_c.reshape(qc.shape[0], qc.shape[1], -1).astype(qc.dtype)


def _dwconv_centred(t, w):
    pad = w.shape[0] // 2
    return lax.conv_general_dilated(t, w[:, None, :].astype(t.dtype), window_strides=(1,), padding=[(pad, pad)],
                                    dimension_numbers=('NWC', 'WIO', 'NWC'), feature_group_count=t.shape[-1])


def _gdn_group(cols_l, cols_c, conv_w, a_log, dt_bias, norm_g):
    def prep(cols):
        q, k, v, b_lr, a_lr, z = cols
        b_, T = q.shape[:2]
        qkv = jax.nn.silu(_dwconv_centred(jnp.concatenate([q, k, v], -1), conv_w))
        q, k, v = jnp.split(qkv, [GDN_HEADS * GDN_DK, 2 * GDN_HEADS * GDN_DK], axis=-1)
        q = _l2norm(q.reshape(b_, T, GDN_HEADS, GDN_DK)) * GDN_DK ** -0.5
        k = _l2norm(k.reshape(b_, T, GDN_HEADS, GDN_DK))
        v = v.reshape(b_, T, GDN_HEADS, GDN_DV)
        beta = jax.nn.sigmoid(b_lr.astype(F32)).reshape(b_, T, 2, GDN_HEADS)
        g = -jnp.exp(a_log.astype(F32)) * jax.nn.softplus(a_lr.astype(F32).reshape(b_, T, 2, GDN_HEADS) + dt_bias)
        return (q, k, v, g[:, :, 0], beta[:, :, 0]), (q, k, v, g[:, :, 1], beta[:, :, 1]), z

    fl, bl, z_l = prep(cols_l)
    fc, bc, z_c = prep(cols_c)
    s0 = jnp.zeros((z_l.shape[0], GDN_HEADS, GDN_DK, GDN_DV), F32)
    o_l, o_c = _bidirectional(_gdn_chunked, fc, bc, fl, bl, s0)
    out = lambda o, z: _rmsnorm(o, norm_g).reshape(z.shape) * jax.nn.silu(z)
    return out(o_l, z_l), out(o_c, z_c)


def _mixer(h_lat, h_ctx, rope, w_in, gla_w_a2, gla_b_a, gla_norm, na_rpb, gdn_conv, gdn_a_log, gdn_dt_bias, gdn_norm, w_out):
    cols_l = _split_cols(h_lat @ w_in)
    cols_c = _split_cols(h_ctx @ w_in)
    gla_l, gla_c = _gla_group(cols_l[0:5], cols_c[0:5], rope, gla_w_a2, gla_b_a, gla_norm)
    na_l, na_c = _na_group(cols_l[5:8], cols_c[5:8], na_rpb)
    gdn_l, gdn_c = _gdn_group(cols_l[8:], cols_c[8:], gdn_conv, gdn_a_log, gdn_dt_bias, gdn_norm)
    out_l = jnp.concatenate([gla_l, na_l, gdn_l], -1) @ w_out
    out_c = jnp.concatenate([gla_c, na_c, gdn_c], -1) @ w_out
    return out_l, out_c


def _moe(h, w_router, b_router, w_gate_up, b_gate_up, w_down, b_down):
    n_tok, d = h.shape
    logits = (h @ w_router + b_router).astype(F32)
    top_logit, top_e = lax.top_k(logits, TOP_K)
    gate = jax.nn.softmax(top_logit, axis=-1)
    nk = n_tok * TOP_K
    flat_e = top_e.reshape(-1).astype(jnp.int32)
    flat_tok = jnp.arange(nk, dtype=jnp.int32) // TOP_K
    order = jnp.argsort(flat_e)
    sorted_e = flat_e[order]
    counts = jnp.bincount(flat_e, length=N_EXPERTS)
    padded = (counts + MOE_BLOCK - 1) // MOE_BLOCK * MOE_BLOCK
    pad_end = jnp.cumsum(padded)
    pad_start = pad_end - padded
    start = jnp.cumsum(counts) - counts
    dest = pad_start[sorted_e] + jnp.arange(nk, dtype=jnp.int32) - start[sorted_e]
    n_blocks = -(-(nk + N_EXPERTS * (MOE_BLOCK - 1)) // MOE_BLOCK)
    buf_tok = jnp.full((n_blocks * MOE_BLOCK,), n_tok, jnp.int32).at[dest].set(flat_tok[order])
    block_e = jnp.minimum(jnp.searchsorted(pad_end, jnp.arange(n_blocks) * MOE_BLOCK, side='right'), N_EXPERTS - 1)
    h_pad = jnp.concatenate([h, jnp.zeros((1, d), h.dtype)], 0)
    xb = h_pad[buf_tok].reshape(n_blocks, MOE_BLOCK, d)

    def expert_block(args):
        xblk, e = args
        gu = xblk @ w_gate_up[e] + b_gate_up[e]
        gt, up = jnp.split(gu, 2, axis=-1)
        gt = jnp.minimum(gt, SWIGLU_LIMIT)
        up = jnp.clip(up, -SWIGLU_LIMIT, SWIGLU_LIMIT)
        act = (up + 1) * gt * jax.nn.sigmoid(SWIGLU_ALPHA * gt)
        return act @ w_down[e] + b_down[e]

    yb = lax.map(expert_block, (xb, block_e)).reshape(n_blocks * MOE_BLOCK, d)
    slot_pos = jnp.zeros((nk,), jnp.int32).at[order].set(dest)
    y = yb[slot_pos].reshape(n_tok, TOP_K, d)
    return jnp.einsum('nkd,nk->nd', y, gate.astype(y.dtype))


def setup_inputs(seed: int = 0) -> dict:
    key = jax.random.key(seed)
    ks = jax.random.split(key, 26)
    it = iter([ks[i] for i in range(26)])
    nrm = lambda shape, std: jax.random.normal(next(it), shape, F32) * std
    L, D, E, F = DEPTH, D_MODEL, N_EXPERTS, D_EXPERT
    x = nrm((BATCH, SEQ, D), 1.0)
    c = nrm((BATCH, D), 1.0)
    ctx = nrm((BATCH, CTX_LEN, D), 1.0)
    c_ctx = nrm((D,), 1.0)
    w_ada = nrm((L, D, N_MOD * D), 0.5 * D ** -0.5)
    b_ada = nrm((L, N_MOD * D), 0.02)
    w_in = nrm((L, D, D_IN), D ** -0.5)
    gla_w_a2 = nrm((L, 2, GLA_RANK, GLA_HEADS * GLA_DK), GLA_RANK ** -0.5)
    gla_b_a = nrm((L, 2, GLA_HEADS * GLA_DK), 0.1)
    gla_norm = 1.0 + nrm((L, GLA_DV), 0.02)
    na_rpb = nrm((L, NA_HEADS, 2 * NA_WIN_R - 1, 2 * NA_WIN_C - 1), 0.02)
    gdn_conv = nrm((L, GDN_CONV, GDN_HEADS * (2 * GDN_DK + GDN_DV)), GDN_CONV ** -0.5)
    gdn_a_log = jnp.log(jax.random.uniform(next(it), (L, 2, GDN_HEADS), F32, 1.0, 16.0))
    dt = jnp.exp(jax.random.uniform(next(it), (L, 2, GDN_HEADS), F32, math.log(1e-3), math.log(1e-1)))
    gdn_dt_bias = dt + jnp.log(-jnp.expm1(-dt))
    gdn_norm = 1.0 + nrm((L, GDN_DV), 0.02)
    w_out = nrm((L, MIX_W, D), DN_BETA * MIX_W ** -0.5)
    ln1_g = 1.0 + nrm((L, D), 0.02)
    ln1_b = nrm((L, D), 0.02)
    w_router = nrm((L, D, E), D ** -0.5)
    b_router = nrm((L, E), 0.01)
    w_gate_up = nrm((L, E, D, 2 * F), D ** -0.5)
    b_gate_up = nrm((L, E, 2 * F), 0.02)
    w_down = nrm((L, E, F, D), DN_BETA * F ** -0.5)
    b_down = nrm((L, E, D), 0.02)
    ln2_g = 1.0 + nrm((L, D), 0.02)
    ln2_b = nrm((L, D), 0.02)
    return {'x': x, 'c': c, 'ctx': ctx, 'c_ctx': c_ctx, 'w_ada': w_ada, 'b_ada': b_ada, 'w_in': w_in,
            'gla_w_a2': gla_w_a2, 'gla_b_a': gla_b_a, 'gla_norm': gla_norm, 'na_rpb': na_rpb,
            'gdn_conv': gdn_conv, 'gdn_a_log': gdn_a_log, 'gdn_dt_bias': gdn_dt_bias, 'gdn_norm': gdn_norm,
            'w_out': w_out, 'ln1_g': ln1_g, 'ln1_b': ln1_b, 'w_router': w_router, 'b_router': b_router,
            'w_gate_up': w_gate_up, 'b_gate_up': b_gate_up, 'w_down': w_down, 'b_down': b_down,
            'ln2_g': ln2_g, 'ln2_b': ln2_b}


def reference(x, c, ctx, c_ctx, w_ada, b_ada, w_in, gla_w_a2, gla_b_a, gla_norm, na_rpb,
              gdn_conv, gdn_a_log, gdn_dt_bias, gdn_norm, w_out, ln1_g, ln1_b,
              w_router, b_router, w_gate_up, b_gate_up, w_down, b_down, ln2_g, ln2_b):
    b_, S, D = x.shape
    rope = _axial_rope_tables(S)
    sc = jax.nn.silu(c)
    scc = jax.nn.silu(c_ctx)
    for l in range(DEPTH):
        m_lat = jnp.split((sc @ w_ada[l] + b_ada[l])[:, None, :], N_MOD, axis=-1)
        m_ctx = jnp.split(scc @ w_ada[l] + b_ada[l], N_MOD, axis=-1)
        a_lat, a_ctx = _mixer(_modulate(x, m_lat[0], m_lat[1]), _modulate(ctx, m_ctx[0], m_ctx[1]), rope,
                              w_in[l], gla_w_a2[l], gla_b_a[l], gla_norm[l], na_rpb[l], gdn_conv[l],
                              gdn_a_log[l], gdn_dt_bias[l], gdn_norm[l], w_out[l])
        x = _post_ln(DN_ALPHA * x + m_lat[2] * a_lat, ln1_g[l], ln1_b[l])
        moe_w = (w_router[l], b_router[l], w_gate_up[l], b_gate_up[l], w_down[l], b_down[l])
        h_lat = _modulate(x, m_lat[3], m_lat[4]).reshape(b_ * S, D)
        if l == DEPTH - 1:
            f_lat = _moe(h_lat, *moe_w)
        else:
            ctx = _post_ln(DN_ALPHA * ctx + m_ctx[2] * a_ctx, ln1_g[l], ln1_b[l])
            h_ctx = _modulate(ctx, m_ctx[3], m_ctx[4]).reshape(-1, D)
            f = _moe(jnp.concatenate([h_lat, h_ctx], 0), *moe_w)
            f_lat = f[:b_ * S]
            ctx = _post_ln(DN_ALPHA * ctx + m_ctx[5] * f[b_ * S:].reshape(ctx.shape), ln2_g[l], ln2_b[l])
        x = _post_ln(DN_ALPHA * x + m_lat[5] * f_lat.reshape(b_, S, D), ln2_g[l], ln2_b[l])
    return x
```

```python
import functools
import math

import jax
import jax.numpy as jnp
import numpy as np
from jax import lax
from jax.experimental import pallas as pl
from jax.experimental.pallas import tpu as pltpu

D_MODEL = 1024
DEPTH = 4
GRID_W = 64
GLA_HEADS, GLA_DK, GLA_DV, GLA_RANK, GLA_TAU, GLA_CHUNK = 4, 32, 64, 16, 16.0, 32
ROPE_BASE = 10000.0
NA_HEADS, NA_HD, NA_WIN_R, NA_WIN_C = 4, 64, 8, 16
GDN_HEADS, GDN_DK, GDN_DV, GDN_CONV, GDN_CHUNK = 4, 128, 128, 5, 64
N_EXPERTS, TOP_K, D_EXPERT = 32, 4, 1024
SWIGLU_LIMIT, SWIGLU_ALPHA = 7.0, 1.702
N_MOD = 6
LN_EPS, RMS_EPS = 1e-5, 1e-6
DN_ALPHA = (2 * DEPTH) ** 0.25

IN_SPLITS = (GLA_HEADS * GLA_DK, GLA_HEADS * GLA_DK, GLA_HEADS * GLA_DV, 2 * GLA_RANK, GLA_HEADS * GLA_DV,
             NA_HEADS * NA_HD, NA_HEADS * NA_HD, NA_HEADS * NA_HD,
             GDN_HEADS * GDN_DK, GDN_HEADS * GDN_DK, GDN_HEADS * GDN_DV, 2 * GDN_HEADS, 2 * GDN_HEADS,
             GDN_HEADS * GDN_DV)
D_IN = sum(IN_SPLITS)
MIX_W = GLA_HEADS * GLA_DV + NA_HEADS * NA_HD + GDN_HEADS * GDN_DV

F32 = jnp.float32
BF16 = jnp.bfloat16

LANES = 128
VMEM_LIMIT_BYTES = 56 * 1024 * 1024
D_IN_PAD = -(-D_IN // LANES) * LANES
ROW_TILE = 512
MOE_BM = 512
F_CHUNK = 512


def _ln_rows(x):
    mu = jnp.mean(x, axis=-1, keepdims=True)
    xc = x - mu
    var = jnp.mean(xc * xc, axis=-1, keepdims=True)
    return xc * lax.rsqrt(var + LN_EPS)


def _inproj_body(x_ref, mod_ref, w_ref, o_ref):
    h = _ln_rows(x_ref[...]) * mod_ref[0, 1:2, :] + mod_ref[0, 0:1, :]
    o_ref[...] = jnp.dot(h.astype(BF16), w_ref[...], preferred_element_type=F32)


def _group_of_block(i, rows_per_group, n_groups):
    return jnp.minimum(i * ROW_TILE // rows_per_group, n_groups - 1)


def _inproj(x, mod, w_bf, rows_per_group):
    n, d = x.shape
    n_groups = mod.shape[0]
    return pl.pallas_call(
        _inproj_body,
        out_shape=jax.ShapeDtypeStruct((n, D_IN_PAD), F32),
        grid=(n // ROW_TILE,),
        in_specs=[
            pl.BlockSpec((ROW_TILE, d), lambda i: (i, 0)),
            pl.BlockSpec((1, 2, d), lambda i: (_group_of_block(i, rows_per_group, n_groups), 0, 0)),
            pl.BlockSpec((d, D_IN_PAD), lambda i: (0, 0)),
        ],
        out_specs=pl.BlockSpec((ROW_TILE, D_IN_PAD), lambda i: (i, 0)),
        compiler_params=pltpu.CompilerParams(dimension_semantics=("arbitrary",),
                                             vmem_limit_bytes=VMEM_LIMIT_BYTES),
        name="inproj",
    )(x, mod, w_bf)


def _outproj_body(mix_ref, x_ref, mod_ref, ln_ref, w_ref, wr_ref, br_ref, xo_ref, h_ref, lg_ref):
    a = jnp.dot(mix_ref[...].astype(BF16), w_ref[...], preferred_element_type=F32)
    y = DN_ALPHA * x_ref[...] + mod_ref[0, 0:1, :] * a
    xn = _ln_rows(y) * ln_ref[0:1, :] + ln_ref[1:2, :]
    xo_ref[...] = xn
    h = _ln_rows(xn) * mod_ref[0, 2:3, :] + mod_ref[0, 1:2, :]
    h_ref[...] = h.astype(BF16)
    lg_ref[...] = jnp.dot(h, wr_ref[...], preferred_element_type=F32,
                          precision=lax.Precision.HIGHEST) + br_ref[...]


def _outproj(mix, x, mod, ln, w_bf, w_router_pad, b_router_pad, rows_per_group):
    n, d = x.shape
    n_groups = mod.shape[0]
    row = lambda i: (i, 0)
    const = lambda i: (0, 0)
    return pl.pallas_call(
        _outproj_body,
        out_shape=(jax.ShapeDtypeStruct((n, d), F32), jax.ShapeDtypeStruct((n, d), BF16),
                   jax.ShapeDtypeStruct((n, LANES), F32)),
        grid=(n // ROW_TILE,),
        in_specs=[
            pl.BlockSpec((ROW_TILE, MIX_W), row),
            pl.BlockSpec((ROW_TILE, d), row),
            pl.BlockSpec((1, 3, d), lambda i: (_group_of_block(i, rows_per_group, n_groups), 0, 0)),
            pl.BlockSpec((2, d), const),
            pl.BlockSpec((MIX_W, d), const),
            pl.BlockSpec((d, LANES), const),
            pl.BlockSpec((1, LANES), const),
        ],
        out_specs=(pl.BlockSpec((ROW_TILE, d), row), pl.BlockSpec((ROW_TILE, d), row),
                   pl.BlockSpec((ROW_TILE, LANES), row)),
        compiler_params=pltpu.CompilerParams(dimension_semantics=("arbitrary",),
                                             vmem_limit_bytes=VMEM_LIMIT_BYTES),
        name="outproj",
    )(mix, x, mod, ln, w_bf, w_router_pad, b_router_pad)


def _moe_body(be_ref, nv_ref, x_ref, wgu_ref, bgu_ref, wd_ref, bd_ref, o_ref, wgu_bf, wd_bf):
    i = pl.program_id(0)
    e = be_ref[i]
    e_prev = be_ref[jnp.maximum(i - 1, 0)]
    valid = i < nv_ref[0]

    @pl.when(valid & ((i == 0) | (e != e_prev)))
    def _():
        wgu_bf[...] = wgu_ref[0, 0].astype(BF16)
        wd_bf[...] = wd_ref[0, 0].astype(BF16)

    @pl.when(valid)
    def _():
        x = x_ref[...]
        acc = jnp.zeros((MOE_BM, D_MODEL), F32) + bd_ref[0, 0]
        for c in range(D_EXPERT // F_CHUNK):
            lo = c * F_CHUNK
            gt = jnp.dot(x, wgu_bf[:, lo:lo + F_CHUNK], preferred_element_type=F32)
            gt = gt + bgu_ref[0, 0, :, lo:lo + F_CHUNK]
            up = jnp.dot(x, wgu_bf[:, D_EXPERT + lo:D_EXPERT + lo + F_CHUNK], preferred_element_type=F32)
            up = up + bgu_ref[0, 0, :, D_EXPERT + lo:D_EXPERT + lo + F_CHUNK]
            gt = jnp.minimum(gt, SWIGLU_LIMIT)
            up = jnp.clip(up, -SWIGLU_LIMIT, SWIGLU_LIMIT)
            act = (up + 1.0) * gt * jax.nn.sigmoid(SWIGLU_ALPHA * gt)
            acc = acc + jnp.dot(act.astype(BF16), wd_bf[lo:lo + F_CHUNK, :], preferred_element_type=F32)
        o_ref[...] = acc


def _moe_experts(layer, block_e, n_valid, xs, w_gate_up, b_gate_up, w_down, b_down):
    n_rows, d = xs.shape
    n_blocks = n_rows // MOE_BM
    f2 = 2 * D_EXPERT

    def blk(i, be, nv):
        return jnp.minimum(i, nv[0] - 1)

    grid_spec = pltpu.PrefetchScalarGridSpec(
        num_scalar_prefetch=2,
        grid=(n_blocks,),
        in_specs=[
            pl.BlockSpec((MOE_BM, d), lambda i, be, nv: (blk(i, be, nv), 0)),
            pl.BlockSpec((1, 1, d, f2), lambda i, be, nv: (layer, be[blk(i, be, nv)], 0, 0)),
            pl.BlockSpec((1, 1, 1, f2), lambda i, be, nv: (layer, be[blk(i, be, nv)], 0, 0)),
            pl.BlockSpec((1, 1, D_EXPERT, d), lambda i, be, nv: (layer, be[blk(i, be, nv)], 0, 0)),
            pl.BlockSpec((1, 1, 1, d), lambda i, be, nv: (layer, be[blk(i, be, nv)], 0, 0)),
        ],
        out_specs=pl.BlockSpec((MOE_BM, d), lambda i, be, nv: (blk(i, be, nv), 0)),
        scratch_shapes=[pltpu.VMEM((d, f2), BF16), pltpu.VMEM((D_EXPERT, d), BF16)],
    )
    return pl.pallas_call(
        _moe_body,
        out_shape=jax.ShapeDtypeStruct((n_rows, d), F32),
        grid_spec=grid_spec,
        compiler_params=pltpu.CompilerParams(dimension_semantics=("arbitrary",),
                                             vmem_limit_bytes=VMEM_LIMIT_BYTES),
        name="moe_experts",
    )(block_e, n_valid, xs, w_gate_up, b_gate_up.reshape(DEPTH, N_EXPERTS, 1, f2), w_down,
      b_down.reshape(DEPTH, N_EXPERTS, 1, d))


def _moe(layer, h_bf, logits, w_gate_up, b_gate_up, w_down, b_down):
    n_tok, d = h_bf.shape
    top_logit, top_e = lax.top_k(logits, TOP_K)
    gate = jax.nn.softmax(top_logit, axis=-1)
    nk = n_tok * TOP_K
    flat_e = top_e.reshape(-1).astype(jnp.int32)
    onehot = (flat_e[:, None] == jnp.arange(N_EXPERTS, dtype=jnp.int32)[None, :]).astype(jnp.int32)
    csum = jnp.cumsum(onehot, axis=0)
    counts = csum[-1]
    rank = jnp.sum((csum - 1) * onehot, axis=1)
    padded = (counts + MOE_BM - 1) // MOE_BM * MOE_BM
    pad_end = jnp.cumsum(padded)
    pad_start = pad_end - padded
    dest = pad_start[flat_e] + rank
    n_blocks = nk // MOE_BM + N_EXPERTS
    flat_tok = jnp.arange(nk, dtype=jnp.int32) // TOP_K
    buf_tok = jnp.zeros((n_blocks * MOE_BM,), jnp.int32).at[dest].set(flat_tok)
    block_start = jnp.arange(n_blocks, dtype=jnp.int32) * MOE_BM
    block_e = jnp.minimum(jnp.searchsorted(pad_end, block_start, side='right'), N_EXPERTS - 1).astype(jnp.int32)
    n_valid = (pad_end[-1] // MOE_BM).astype(jnp.int32).reshape(1)
    xs = h_bf[buf_tok]
    yb = _moe_experts(layer, block_e, n_valid, xs, w_gate_up, b_gate_up, w_down, b_down)
    y = yb[dest].reshape(n_tok, TOP_K, d)
    return jnp.sum(y * gate[..., None], axis=1)


def _rmsnorm(t, g):
    y = t * lax.rsqrt(jnp.mean(t * t, -1, keepdims=True) + RMS_EPS)
    return y * g


def _l2norm(t):
    return t * lax.rsqrt(jnp.sum(t * t, -1, keepdims=True) + 1e-6)


def _axial_rope_tables(n_tok):
    t = jnp.arange(n_tok)
    half = GLA_DK // 2
    inv_freq = ROPE_BASE ** (-jnp.arange(0, half, 2, dtype=F32) / half)

    def cs(pos):
        ang = pos.astype(F32)[:, None] * inv_freq
        return jnp.cos(ang)[:, None, :], jnp.sin(ang)[:, None, :]

    return cs(t // GRID_W) + cs(t % GRID_W)


def _rotate(t, cos, sin):
    t1, t2 = jnp.split(t, 2, -1)
    return jnp.concatenate([t1 * cos - t2 * sin, t2 * cos + t1 * sin], -1)


def _axial_rope(t, rot):
    cos_r, sin_r, cos_c, sin_c = rot
    tr, tc = jnp.split(t, 2, -1)
    return jnp.concatenate([_rotate(tr, cos_r, sin_r), _rotate(tc, cos_c, sin_c)], -1)


def _blocks(t, chunk):
    b_, T, H = t.shape[:3]
    return t.reshape(b_, T // chunk, chunk, H, -1).transpose(1, 0, 3, 2, 4)


def _unblocks(o):
    n, b_, H, C, d = o.shape
    return o.transpose(1, 0, 3, 2, 4).reshape(b_, n * C, H, d)


def _gla_chunked(q, k, v, log_a, s0):
    C = GLA_CHUNK
    qc, kc, vc = _blocks(q, C), _blocks(k, C), _blocks(v, C)
    gc = jnp.cumsum(_blocks(log_a, C), axis=3)
    tril = jnp.tril(jnp.ones((C, C), dtype=bool))
    diff = jnp.where(tril[:, :, None], gc[..., :, None, :] - gc[..., None, :, :], -jnp.inf)
    attn = jnp.einsum('nbhtd,nbhsd,nbhtsd->nbhts', qc, kc, jnp.exp(diff))
    o_intra = jnp.einsum('nbhts,nbhsv->nbhtv', attn, vc)
    q_dec = qc * jnp.exp(gc)
    k_dec = kc * jnp.exp(gc[..., -1:, :] - gc)
    a_last = jnp.exp(gc[..., -1, :])

    def step(s, inp):
        qd, kd, vv, al = inp
        o = jnp.einsum('bhtk,bhkv->bhtv', qd, s)
        s = al[..., None] * s + jnp.einsum('bhtk,bhtv->bhkv', kd, vv)
        return s, o

    s_T, o_inter = lax.scan(step, s0, (q_dec, k_dec, vc, a_last))
    return _unblocks(o_intra + o_inter), s_T


def _gdn_chunked(q, k, v, g, beta, s0):
    C = GDN_CHUNK
    dv = v.shape[-1]
    qc, kc, vc = _blocks(q, C), _blocks(k, C), _blocks(v, C)
    gc = jnp.cumsum(_blocks(g[..., None], C)[..., 0], axis=-1)
    bc = _blocks(beta[..., None], C)[..., 0]
    tril = jnp.tril(jnp.ones((C, C), dtype=bool))
    strict = jnp.tril(jnp.ones((C, C), dtype=bool), -1)
    decay = jnp.exp(jnp.where(tril, gc[..., :, None] - gc[..., None, :], -jnp.inf))
    kb = kc * bc[..., None]
    a_mat = jnp.where(strict, jnp.einsum('nbhtk,nbhsk->nbhts', kb, kc) * decay, 0.0) + jnp.eye(C, dtype=F32)
    rhs = jnp.concatenate([vc * bc[..., None], kb * jnp.exp(gc)[..., None]], axis=-1)
    sol = lax.linalg.triangular_solve(a_mat, rhs, left_side=True, lower=True)
    u, w = sol[..., :dv], sol[..., dv:]
    qk = jnp.where(tril, jnp.einsum('nbhtk,nbhsk->nbhts', qc, kc) * decay, 0.0)
    q_dec = qc * jnp.exp(gc)[..., None]
    k_dec = kc * jnp.exp(gc[..., -1:] - gc)[..., None]
    a_last = jnp.exp(gc[..., -1])

    def step(s, inp):
        qd, kd, uu, ww, qkk, al = inp
        v_new = uu - jnp.einsum('bhtk,bhkv->bhtv', ww, s)
        o = jnp.einsum('bhtk,bhkv->bhtv', qd, s) + jnp.einsum('bhts,bhsv->bhtv', qkk, v_new)
        s = al[..., None, None] * s + jnp.einsum('bhtk,bhtv->bhkv', kd, v_new)
        return s, o

    s_T, o = lax.scan(step, s0, (q_dec, k_dec, u, w, qk, a_last))
    return _unblocks(o), s_T


def _bidirectional(scan, ctx_fwd, ctx_bwd, lat_fwd, lat_bwd, s0):
    flip = lambda args: tuple(jnp.flip(a, 1) for a in args)
    o_cf, s_f = scan(*ctx_fwd, s0)
    o_cb, s_b = scan(*flip(ctx_bwd), s0)
    o_lf, _ = scan(*lat_fwd, s_f)
    o_lb, _ = scan(*flip(lat_bwd), s_b)
    return o_lf + jnp.flip(o_lb, 1), o_cf + jnp.flip(o_cb, 1)


def _gla_group(cols_l, cols_c, rope, w_a2, b_a, norm_g):
    def prep(cols, rot):
        q, k, v, a_lr, g = cols
        b_, T = q.shape[:2]
        q = q.reshape(b_, T, GLA_HEADS, GLA_DK) * GLA_DK ** -0.5
        k = k.reshape(b_, T, GLA_HEADS, GLA_DK)
        if rot is not None:
            q, k = _axial_rope(q, rot), _axial_rope(k, rot)
        v = v.reshape(b_, T, GLA_HEADS, GLA_DV)
        z = jnp.einsum('btir,irk->btik', a_lr.reshape(b_, T, 2, GLA_RANK), w_a2) + b_a
        log_a = (jax.nn.log_sigmoid(z) / GLA_TAU).reshape(b_, T, 2, GLA_HEADS, GLA_DK)
        return (q, k, v, log_a[:, :, 0]), (q, k, v, log_a[:, :, 1]), g

    fl, bl, g_l = prep(cols_l, rope)
    fc, bc, g_c = prep(cols_c, None)
    s0 = jnp.zeros((g_l.shape[0], GLA_HEADS, GLA_DK, GLA_DV), F32)
    o_l, o_c = _bidirectional(_gla_chunked, fc, bc, fl, bl, s0)
    out = lambda o, g: _rmsnorm(o, norm_g).reshape(g.shape) * jax.nn.silu(g)
    return out(o_l, g_l), out(o_c, g_c)


def _neighbourhood_attn(q, k, v, k_ctx, v_ctx, rpb):
    b_, S, H, hd = q.shape
    rows = S // GRID_W
    kr = min(NA_WIN_R, rows)
    scale = hd ** -0.5
    grid = lambda t: t.reshape(b_, rows, GRID_W, H, hd)
    qg, kg, vg = grid(q), grid(k), grid(v)
    r = jnp.arange(rows)
    r0 = jnp.clip(r - kr // 2, 0, rows - kr)
    key_rows = r0[:, None] + jnp.arange(kr)
    kb = kg[:, key_rows]
    vb = vg[:, key_rows].reshape(b_, rows, kr * GRID_W, H, hd)
    s_loc = jnp.einsum('brqhd,brikhd->bhrqik', qg, kb) * scale
    cidx = jnp.arange(GRID_W)
    c0 = jnp.clip(cidx - NA_WIN_C // 2, 0, GRID_W - NA_WIN_C)
    col_ok = (cidx[None, :] >= c0[:, None]) & (cidx[None, :] < c0[:, None] + NA_WIN_C)
    dr = key_rows - r[:, None]
    dc = jnp.clip(cidx[None, :] - cidx[:, None] + NA_WIN_C - 1, 0, 2 * NA_WIN_C - 2)
    bias = rpb[:, (dr + NA_WIN_R - 1)[:, None, :, None], dc[None, :, None, :]]
    s_loc = jnp.where(col_ok[:, None, :], s_loc + bias, -jnp.inf).reshape(b_, H, rows, GRID_W, kr * GRID_W)
    s_ctx = jnp.einsum('brqhd,blhd->bhrql', qg, k_ctx) * scale
    p = jax.nn.softmax(jnp.concatenate([s_loc, s_ctx], -1), axis=-1)
    n_loc = kr * GRID_W
    o = (jnp.einsum('bhrqj,brjhd->brqhd', p[..., :n_loc], vb)
         + jnp.einsum('bhrql,blhd->brqhd', p[..., n_loc:], v_ctx))
    return o.reshape(b_, S, H * hd)


def _na_group(cols_l, cols_c, rpb):
    heads = lambda t: t.reshape(t.shape[0], t.shape[1], NA_HEADS, NA_HD)
    ql, kl, vl = (heads(t) for t in cols_l)
    qc, kc, vc = (heads(t) for t in cols_c)
    o_l = _neighbourhood_attn(ql, kl, vl, kc, vc, rpb)
    s = jnp.einsum('bqhd,bkhd->bhqk', qc, kc) * NA_HD ** -0.5
    o_c = jnp.einsum('bhqk,bkhd->bqhd', jax.nn.softmax(s, axis=-1), vc)
    return o_l, o_c.reshape(qc.shape[0], qc.shape[1], -1)


def _dwconv_centred(t, w):
    pad = w.shape[0] // 2
    return lax.conv_general_dilated(t, w[:, None, :], window_strides=(1,), padding=[(pad, pad)],
                                    dimension_numbers=('NWC', 'WIO', 'NWC'), feature_group_count=t.shape[-1])


def _gdn_group(cols_l, cols_c, conv_w, a_log, dt_bias, norm_g):
    def prep(cols):
        q, k, v, b_lr, a_lr, z = cols
        b_, T = q.shape[:2]
        qkv = jax.nn.silu(_dwconv_centred(jnp.concatenate([q, k, v], -1), conv_w))
        q, k, v = jnp.split(qkv, [GDN_HEADS * GDN_DK, 2 * GDN_HEADS * GDN_DK], axis=-1)
        q = _l2norm(q.reshape(b_, T, GDN_HEADS, GDN_DK)) * GDN_DK ** -0.5
        k = _l2norm(k.reshape(b_, T, GDN_HEADS, GDN_DK))
        v = v.reshape(b_, T, GDN_HEADS, GDN_DV)
        beta = jax.nn.sigmoid(b_lr).reshape(b_, T, 2, GDN_HEADS)
        g = -jnp.exp(a_log) * jax.nn.softplus(a_lr.reshape(b_, T, 2, GDN_HEADS) + dt_bias)
        return (q, k, v, g[:, :, 0], beta[:, :, 0]), (q, k, v, g[:, :, 1], beta[:, :, 1]), z

    fl, bl, z_l = prep(cols_l)
    fc, bc, z_c = prep(cols_c)
    s0 = jnp.zeros((z_l.shape[0], GDN_HEADS, GDN_DK, GDN_DV), F32)
    o_l, o_c = _bidirectional(_gdn_chunked, fc, bc, fl, bl, s0)
    out = lambda o, z: _rmsnorm(o, norm_g).reshape(z.shape) * jax.nn.silu(z)
    return out(o_l, z_l), out(o_c, z_c)


def _split_cols(p):
    return jnp.split(p[..., :D_IN], np.cumsum(IN_SPLITS)[:-1].tolist(), axis=-1)


def _mixers(cols, b_, S, L, rope, gla_w_a2, gla_b_a, gla_norm, na_rpb, gdn_conv, gdn_a_log, gdn_dt_bias, gdn_norm):
    cols_l = _split_cols(cols[:b_ * S].reshape(b_, S, -1))
    cols_c = _split_cols(cols[b_ * S:].reshape(b_, L, -1))
    gla_l, gla_c = _gla_group(cols_l[0:5], cols_c[0:5], rope, gla_w_a2, gla_b_a, gla_norm)
    na_l, na_c = _na_group(cols_l[5:8], cols_c[5:8], na_rpb)
    gdn_l, gdn_c = _gdn_group(cols_l[8:], cols_c[8:], gdn_conv, gdn_a_log, gdn_dt_bias, gdn_norm)
    mix_l = jnp.concatenate([gla_l, na_l, gdn_l], -1).reshape(b_ * S, MIX_W)
    mix_c = jnp.concatenate([gla_c, na_c, gdn_c], -1).reshape(b_ * L, MIX_W)
    return jnp.concatenate([mix_l, mix_c], 0)


def _post_ln(t, g, b):
    return _ln_rows(t) * g + b


def kernel(x, c, ctx, c_ctx, w_ada, b_ada, w_in, gla_w_a2, gla_b_a, gla_norm, na_rpb, gdn_conv, gdn_a_log,
           gdn_dt_bias, gdn_norm, w_out, ln1_g, ln1_b, w_router, b_router, w_gate_up, b_gate_up, w_down, b_down,
           ln2_g, ln2_b):
    b_, S, D = x.shape
    L = ctx.shape[1]
    n_lat, n_ctx = b_ * S, b_ * L
    rope = _axial_rope_tables(S)
    sc = jnp.concatenate([jax.nn.silu(c), jax.nn.silu(c_ctx)[None, :]], 0)
    xt = jnp.concatenate([x.reshape(n_lat, D), ctx.reshape(n_ctx, D)], 0)
    for l in range(DEPTH):
        last = l == DEPTH - 1
        m = (jnp.dot(sc, w_ada[l], precision=lax.Precision.HIGHEST) + b_ada[l]).reshape(b_ + 1, N_MOD, D)
        shift1, scale1, gate1, shift2, scale2, gate2 = (m[:, j] for j in range(N_MOD))
        w_in_bf = jnp.pad(w_in[l], ((0, 0), (0, D_IN_PAD - D_IN))).astype(BF16)
        cols = _inproj(xt, jnp.stack([shift1, 1.0 + scale1], 1), w_in_bf, S)
        mix = _mixers(cols, b_, S, L, rope, gla_w_a2[l], gla_b_a[l], gla_norm[l], na_rpb[l], gdn_conv[l],
                      gdn_a_log[l], gdn_dt_bias[l], gdn_norm[l])
        n_rows = n_lat if last else n_lat + n_ctx
        w_router_pad = jnp.pad(w_router[l], ((0, 0), (0, LANES - N_EXPERTS)))
        b_router_pad = jnp.pad(b_router[l], (0, LANES - N_EXPERTS)).reshape(1, LANES)
        x1, h2, logits = _outproj(mix[:n_rows], xt[:n_rows], jnp.stack([gate1, shift2, 1.0 + scale2], 1),
                                  jnp.stack([ln1_g[l], ln1_b[l]], 0), w_out[l].astype(BF16),
                                  w_router_pad, b_router_pad, S)
        f = _moe(l, h2, logits[:, :N_EXPERTS], w_gate_up, b_gate_up, w_down, b_down)
        gate2_rows = jnp.concatenate([jnp.repeat(gate2[:b_], S, axis=0),
                                      jnp.broadcast_to(gate2[b_], (n_ctx, D))], 0)[:n_rows]
        xt = _post_ln(DN_ALPHA * x1 + gate2_rows * f, ln2_g[l], ln2_b[l])
    return xt[:n_lat].reshape(b_, S, D)
```

```python
import numpy as np

import jax
import jax.numpy as jnp
from jax import lax
from jax.experimental import pallas as pl
from jax.experimental.pallas import tpu as pltpu

D_MODEL = 1024
DEPTH = 4
GRID_W = 64
GLA_HEADS, GLA_DK, GLA_DV, GLA_RANK, GLA_TAU, GLA_CHUNK = 4, 32, 64, 16, 16.0, 32
ROPE_BASE = 10000.0
NA_HEADS, NA_HD, NA_WIN_R, NA_WIN_C = 4, 64, 8, 16
GDN_HEADS, GDN_DK, GDN_DV, GDN_CONV, GDN_CHUNK = 4, 128, 128, 5, 64
N_EXPERTS, TOP_K, D_EXPERT = 32, 4, 1024
SWIGLU_LIMIT, SWIGLU_ALPHA = 7.0, 1.702
N_MOD = 6
LN_EPS, RMS_EPS = 1e-5, 1e-6
DN_ALPHA = (2 * DEPTH) ** 0.25

F32 = jnp.float32
BF16 = jnp.bfloat16

LANES = 128
VMEM_LIMIT_BYTES = 56 * 1024 * 1024
ROW_TILE = 512
MOE_BM = 512
F_CHUNK = 512

GLA_QK_W = GLA_HEADS * GLA_DK
GLA_V_W = GLA_HEADS * GLA_DV
NA_W = NA_HEADS * NA_HD
GDN_QK_W = GDN_HEADS * GDN_DK
GDN_V_W = GDN_HEADS * GDN_DV
GLA_SEG_W = 2 * GLA_QK_W + 2 * GLA_V_W + LANES
NA_SEG_W = 3 * NA_W
GDN_SEG_W = 2 * GDN_QK_W + 2 * GDN_V_W + LANES
D_IN_SEG = GLA_SEG_W + NA_SEG_W + GDN_SEG_W
MIX_W = GLA_V_W + NA_W + GDN_V_W

_NT = (((1,), (1,)), ((), ()))
_TN = (((0,), (0,)), ((), ()))
_HI = lax.Precision.HIGHEST


def _ln_rows(x):
    mu = jnp.mean(x, axis=-1, keepdims=True)
    xc = x - mu
    var = jnp.mean(xc * xc, axis=-1, keepdims=True)
    return xc * lax.rsqrt(var + LN_EPS)


def _group_of_block(i, rows_per_group, n_groups):
    return jnp.minimum(i * ROW_TILE // rows_per_group, n_groups - 1)


def _inproj_weights(w):
    o = np.cumsum((0, GLA_QK_W, GLA_QK_W, GLA_V_W, 2 * GLA_RANK, GLA_V_W, NA_W, NA_W, NA_W,
                   GDN_QK_W, GDN_QK_W, GDN_V_W, 2 * GDN_HEADS, 2 * GDN_HEADS, GDN_V_W)).tolist()
    col = lambda i: w[:, o[i]:o[i + 1]]
    zeros = lambda n: jnp.zeros((w.shape[0], n), w.dtype)
    parts = [col(0), col(1), col(2), col(4), col(3), zeros(LANES - 2 * GLA_RANK),
             col(5), col(6), col(7),
             col(8), col(9), col(10), col(13), col(11), col(12), zeros(LANES - 4 * GDN_HEADS)]
    return jnp.concatenate(parts, axis=1).astype(BF16)


def _inproj_body(x_ref, mod_ref, w_ref, gla_ref, na_ref, gdn_ref):
    h = _ln_rows(x_ref[...]) * mod_ref[0, 1:2, :] + mod_ref[0, 0:1, :]
    hb = h.astype(BF16)
    gla_ref[...] = jnp.dot(hb, w_ref[:, 0:GLA_SEG_W], preferred_element_type=F32)
    na_ref[...] = jnp.dot(hb, w_ref[:, GLA_SEG_W:GLA_SEG_W + NA_SEG_W], preferred_element_type=F32).astype(BF16)
    gdn_ref[...] = jnp.dot(hb, w_ref[:, GLA_SEG_W + NA_SEG_W:], preferred_element_type=F32)


def _inproj(x, mod, w_bf, rows_per_group):
    n, d = x.shape
    n_groups = mod.shape[0]
    row = lambda i: (i, 0)
    return pl.pallas_call(
        _inproj_body,
        out_shape=(jax.ShapeDtypeStruct((n, GLA_SEG_W), F32), jax.ShapeDtypeStruct((n, NA_SEG_W), BF16),
                   jax.ShapeDtypeStruct((n, GDN_SEG_W), F32)),
        grid=(n // ROW_TILE,),
        in_specs=[
            pl.BlockSpec((ROW_TILE, d), row),
            pl.BlockSpec((1, 2, d), lambda i: (_group_of_block(i, rows_per_group, n_groups), 0, 0)),
            pl.BlockSpec((d, D_IN_SEG), lambda i: (0, 0)),
        ],
        out_specs=(pl.BlockSpec((ROW_TILE, GLA_SEG_W), row), pl.BlockSpec((ROW_TILE, NA_SEG_W), row),
                   pl.BlockSpec((ROW_TILE, GDN_SEG_W), row)),
        compiler_params=pltpu.CompilerParams(dimension_semantics=("arbitrary",),
                                             vmem_limit_bytes=VMEM_LIMIT_BYTES),
        name="inproj",
    )(x, mod, w_bf)


def _outproj_body(gf_ref, gb_ref, gg_ref, gn_ref, na_ref, gdn_ref, x_ref, mod_ref, ln_ref, w_ref, wr_ref, br_ref,
                  xo_ref, h_ref, lg_ref):
    o = gf_ref[...] + gb_ref[...]
    sq = o * o
    sq_hi = sq.astype(BF16)
    sq_lo = (sq - sq_hi.astype(F32)).astype(BF16)
    head_mean = ((lax.broadcasted_iota(jnp.int32, (GLA_V_W, GLA_V_W), 0) // GLA_DV)
                 == (lax.broadcasted_iota(jnp.int32, (GLA_V_W, GLA_V_W), 1) // GLA_DV)
                 ).astype(BF16) * (1.0 / GLA_DV)
    ms = (jnp.dot(sq_hi, head_mean, preferred_element_type=F32)
          + jnp.dot(sq_lo, head_mean, preferred_element_type=F32))
    gla = o * lax.rsqrt(ms + RMS_EPS) * gn_ref[...] * jax.nn.silu(gg_ref[...])
    a = (jnp.dot(gla.astype(BF16), w_ref[0:GLA_V_W, :], preferred_element_type=F32)
         + jnp.dot(na_ref[...], w_ref[GLA_V_W:GLA_V_W + NA_W, :], preferred_element_type=F32)
         + jnp.dot(gdn_ref[...].astype(BF16), w_ref[GLA_V_W + NA_W:, :], preferred_element_type=F32))
    y = DN_ALPHA * x_ref[...] + mod_ref[0, 0:1, :] * a
    xn = _ln_rows(y) * ln_ref[0:1, :] + ln_ref[1:2, :]
    xo_ref[...] = xn
    h = _ln_rows(xn) * mod_ref[0, 2:3, :] + mod_ref[0, 1:2, :]
    h_ref[...] = h
    lg_ref[...] = jnp.dot(h, wr_ref[...], preferred_element_type=F32, precision=_HI) + br_ref[...]


def _outproj(n, gla_f, gla_b, gla_cols, gla_norm, na, gdn, x, mod, ln, w_bf, w_router_pad, b_router_pad,
             rows_per_group):
    d = x.shape[1]
    n_groups = mod.shape[0]
    row = lambda i: (i, 0)
    const = lambda i: (0, 0)
    return pl.pallas_call(
        _outproj_body,
        out_shape=(jax.ShapeDtypeStruct((n, d), F32), jax.ShapeDtypeStruct((n, d), F32),
                   jax.ShapeDtypeStruct((n, LANES), F32)),
        grid=(n // ROW_TILE,),
        in_specs=[
            pl.BlockSpec((ROW_TILE, GLA_V_W), row),
            pl.BlockSpec((ROW_TILE, GLA_V_W), row),
            pl.BlockSpec((ROW_TILE, GLA_V_W), lambda i: (i, (2 * GLA_QK_W + GLA_V_W) // GLA_V_W)),
            pl.BlockSpec((1, GLA_V_W), const),
            pl.BlockSpec((ROW_TILE, NA_W), row),
            pl.BlockSpec((ROW_TILE, GDN_V_W), row),
            pl.BlockSpec((ROW_TILE, d), row),
            pl.BlockSpec((1, 3, d), lambda i: (_group_of_block(i, rows_per_group, n_groups), 0, 0)),
            pl.BlockSpec((2, d), const),
            pl.BlockSpec((MIX_W, d), const),
            pl.BlockSpec((d, LANES), const),
            pl.BlockSpec((1, LANES), const),
        ],
        out_specs=(pl.BlockSpec((ROW_TILE, d), row), pl.BlockSpec((ROW_TILE, d), row),
                   pl.BlockSpec((ROW_TILE, LANES), row)),
        compiler_params=pltpu.CompilerParams(dimension_semantics=("arbitrary",),
                                             vmem_limit_bytes=VMEM_LIMIT_BYTES),
        name="outproj",
    )(gla_f, gla_b, gla_cols, gla_norm, na, gdn, x, mod, ln, w_bf, w_router_pad, b_router_pad)


def _moe_body(be_ref, nv_ref, x_ref, wgu_ref, bgu_ref, wd_ref, bd_ref, o_ref, wgu_bf, wd_bf):
    i = pl.program_id(0)
    e = be_ref[i]
    e_prev = be_ref[jnp.maximum(i - 1, 0)]
    valid = i < nv_ref[0]

    @pl.when(valid & ((i == 0) | (e != e_prev)))
    def _():
        wgu_bf[...] = wgu_ref[0, 0].astype(BF16)
        wd_bf[...] = wd_ref[0, 0].astype(BF16)

    @pl.when(valid)
    def _():
        x = x_ref[...].astype(BF16)
        acc = jnp.zeros((MOE_BM, D_MODEL), F32) + bd_ref[0, 0]
        for c in range(D_EXPERT // F_CHUNK):
            lo = c * F_CHUNK
            gt = jnp.dot(x, wgu_bf[:, lo:lo + F_CHUNK], preferred_element_type=F32)
            gt = gt + bgu_ref[0, 0, :, lo:lo + F_CHUNK]
            up = jnp.dot(x, wgu_bf[:, D_EXPERT + lo:D_EXPERT + lo + F_CHUNK], preferred_element_type=F32)
            up = up + bgu_ref[0, 0, :, D_EXPERT + lo:D_EXPERT + lo + F_CHUNK]
            gt = jnp.minimum(gt, SWIGLU_LIMIT)
            up = jnp.clip(up, -SWIGLU_LIMIT, SWIGLU_LIMIT)
            act = (up + 1.0) * gt * jax.nn.sigmoid(SWIGLU_ALPHA * gt)
            acc = acc + jnp.dot(act.astype(BF16), wd_bf[lo:lo + F_CHUNK, :], preferred_element_type=F32)
        o_ref[...] = acc

    @pl.when(jnp.logical_not(valid))
    def _():
        o_ref[...] = jnp.zeros_like(o_ref)


def _moe_experts(layer, block_e, n_valid, xs, w_gate_up, b_gate_up, w_down, b_down):
    n_rows, d = xs.shape
    n_blocks = n_rows // MOE_BM
    f2 = 2 * D_EXPERT

    def blk(i, be, nv):
        return jnp.minimum(i, nv[0] - 1)

    grid_spec = pltpu.PrefetchScalarGridSpec(
        num_scalar_prefetch=2,
        grid=(n_blocks,),
        in_specs=[
            pl.BlockSpec((MOE_BM, d), lambda i, be, nv: (blk(i, be, nv), 0)),
            pl.BlockSpec((1, 1, d, f2), lambda i, be, nv: (layer, be[blk(i, be, nv)], 0, 0)),
            pl.BlockSpec((1, 1, 1, f2), lambda i, be, nv: (layer, be[blk(i, be, nv)], 0, 0)),
            pl.BlockSpec((1, 1, D_EXPERT, d), lambda i, be, nv: (layer, be[blk(i, be, nv)], 0, 0)),
            pl.BlockSpec((1, 1, 1, d), lambda i, be, nv: (layer, be[blk(i, be, nv)], 0, 0)),
        ],
        out_specs=pl.BlockSpec((MOE_BM, d), lambda i, be, nv: (i, 0)),
        scratch_shapes=[pltpu.VMEM((d, f2), BF16), pltpu.VMEM((D_EXPERT, d), BF16)],
    )
    return pl.pallas_call(
        _moe_body,
        out_shape=jax.ShapeDtypeStruct((n_rows, d), F32),
        grid_spec=grid_spec,
        compiler_params=pltpu.CompilerParams(dimension_semantics=("arbitrary",),
                                             vmem_limit_bytes=VMEM_LIMIT_BYTES),
        name="moe_experts",
    )(block_e, n_valid, xs, w_gate_up, b_gate_up.reshape(DEPTH, N_EXPERTS, 1, f2), w_down,
      b_down.reshape(DEPTH, N_EXPERTS, 1, d))


GATHER_ROWS = 256


def _gather_body(idx_ref, src_ref, o_ref, sem):
    i = pl.program_id(0)
    slot = lax.rem(i, 2)

    def row_copy(src_row, dst_row, s):
        return pltpu.make_async_copy(src_ref.at[pl.ds(src_row, 1)], o_ref.at[pl.ds(dst_row, 1)], sem.at[s])

    def wait_step(s):
        for _ in range(GATHER_ROWS):
            row_copy(0, 0, s).wait()

    for r in range(GATHER_ROWS):
        row_copy(idx_ref[0, 0, r], i * GATHER_ROWS + r, slot).start()

    @pl.when(i > 0)
    def _():
        wait_step(1 - slot)

    @pl.when(i == pl.num_programs(0) - 1)
    def _():
        wait_step(slot)


def _gather_rows(src, idx):
    n_out = idx.shape[0]
    n_steps = n_out // GATHER_ROWS
    return pl.pallas_call(
        _gather_body,
        out_shape=jax.ShapeDtypeStruct((n_out, src.shape[1]), src.dtype),
        grid=(n_steps,),
        in_specs=[
            pl.BlockSpec((1, 1, GATHER_ROWS), lambda i: (i, 0, 0), memory_space=pltpu.SMEM),
            pl.BlockSpec(memory_space=pl.ANY),
        ],
        out_specs=pl.BlockSpec(memory_space=pl.ANY),
        scratch_shapes=[pltpu.SemaphoreType.DMA((2,))],
        compiler_params=pltpu.CompilerParams(dimension_semantics=("arbitrary",)),
        name="gather_rows",
    )(idx.reshape(n_steps, 1, GATHER_ROWS), src)


def _combine_body(y_ref, g_ref, x_ref, mod_ref, ln_ref, o_ref):
    d = x_ref.shape[1]
    g = g_ref[...]
    f = g[:, 0:1] * y_ref[:, 0:d]
    for k in range(1, TOP_K):
        f = f + g[:, k:k + 1] * y_ref[:, k * d:(k + 1) * d]
    y = DN_ALPHA * x_ref[...] + mod_ref[0, 0:1, :] * f
    o_ref[...] = _ln_rows(y) * ln_ref[0:1, :] + ln_ref[1:2, :]


def _combine(y4, gate, x1, gate2, ln, rows_per_group):
    n, d = x1.shape
    n_groups = gate2.shape[0]
    row = lambda i: (i, 0)
    return pl.pallas_call(
        _combine_body,
        out_shape=jax.ShapeDtypeStruct((n, d), F32),
        grid=(n // ROW_TILE,),
        in_specs=[
            pl.BlockSpec((ROW_TILE, TOP_K * d), row),
            pl.BlockSpec((ROW_TILE, TOP_K), row),
            pl.BlockSpec((ROW_TILE, d), row),
            pl.BlockSpec((1, 1, d), lambda i: (_group_of_block(i, rows_per_group, n_groups), 0, 0)),
            pl.BlockSpec((2, d), lambda i: (0, 0)),
        ],
        out_specs=pl.BlockSpec((ROW_TILE, d), row),
        compiler_params=pltpu.CompilerParams(dimension_semantics=("arbitrary",),
                                             vmem_limit_bytes=VMEM_LIMIT_BYTES),
        name="combine",
    )(y4, gate, x1, gate2, ln)


def _moe(layer, h, logits, w_gate_up, b_gate_up, w_down, b_down):
    n_tok, d = h.shape
    top_logit, top_e = lax.top_k(logits, TOP_K)
    gate = jax.nn.softmax(top_logit, axis=-1)
    nk = n_tok * TOP_K
    flat_e = top_e.reshape(-1).astype(jnp.int32)
    onehot = (flat_e[:, None] == jnp.arange(N_EXPERTS, dtype=jnp.int32)[None, :]).astype(jnp.int32)
    csum = jnp.cumsum(onehot, axis=0)
    counts = csum[-1]
    rank = jnp.sum((csum - 1) * onehot, axis=1)
    padded = (counts + MOE_BM - 1) // MOE_BM * MOE_BM
    pad_end = jnp.cumsum(padded)
    pad_start = pad_end - padded
    dest = pad_start[flat_e] + rank
    n_blocks = nk // MOE_BM + N_EXPERTS
    flat_tok = jnp.arange(nk, dtype=jnp.int32) // TOP_K
    buf_tok = jnp.zeros((n_blocks * MOE_BM,), jnp.int32).at[dest].set(flat_tok)
    block_start = jnp.arange(n_blocks, dtype=jnp.int32) * MOE_BM
    block_e = jnp.minimum(jnp.searchsorted(pad_end, block_start, side='right'), N_EXPERTS - 1).astype(jnp.int32)
    n_valid = (pad_end[-1] // MOE_BM).astype(jnp.int32).reshape(1)
    xs = _gather_rows(h, buf_tok)
    yb = _moe_experts(layer, block_e, n_valid, xs, w_gate_up, b_gate_up, w_down, b_down)
    return _gather_rows(yb, dest).reshape(n_tok, TOP_K * d), gate


NA_ROWS_PER_STEP = 4
NA_PATTERN_ROWS = (0, 1, 2, 3, GRID_W // 2, GRID_W - 3, GRID_W - 2, GRID_W - 1)


def _na_bias_table(rpb):
    rows = GRID_W
    pat_r = np.array(NA_PATTERN_ROWS)
    r0 = np.clip(pat_r - NA_WIN_R // 2, 0, rows - NA_WIN_R)
    dr = r0[:, None] + np.arange(NA_WIN_R)[None, :] - pat_r[:, None]
    cidx = np.arange(GRID_W)
    c0 = np.clip(cidx - NA_WIN_C // 2, 0, GRID_W - NA_WIN_C)
    col_ok = (cidx[None, :] >= c0[:, None]) & (cidx[None, :] < c0[:, None] + NA_WIN_C)
    dc = np.clip(cidx[None, :] - cidx[:, None] + NA_WIN_C - 1, 0, 2 * NA_WIN_C - 2)
    bias = rpb[:, (dr + NA_WIN_R - 1)[:, None, :, None], dc[None, :, None, :]]
    bias = jnp.where(col_ok[None, None, :, None, :], bias, -jnp.inf)
    return bias.transpose(1, 0, 2, 3, 4).reshape(len(NA_PATTERN_ROWS), NA_HEADS * GRID_W, NA_WIN_R * GRID_W)


def _head_block_mask(n_q):
    shape = (NA_HEADS * n_q, NA_W)
    return (lax.broadcasted_iota(jnp.int32, shape, 0) // n_q) == (lax.broadcasted_iota(jnp.int32, shape, 1) // NA_HD)


def _heads_on_rows(q, mask):
    qt = jnp.concatenate([q] * NA_HEADS, axis=0)
    return jnp.where(mask, qt, jnp.zeros_like(qt)) * (NA_HD ** -0.5)


def _heads_to_lanes(o_all, n_q, n_heads, head_w):
    lane_h = lax.broadcasted_iota(jnp.int32, (n_q, n_heads * head_w), 1) // head_w
    out = jnp.zeros((n_q, n_heads * head_w), F32)
    for h in range(n_heads):
        out = out + jnp.where(lane_h == h, o_all[h * n_q:(h + 1) * n_q], 0.0)
    return out


def _na_body(q_ref, k_ref, v_ref, kc_ref, vc_ref, bias_ref, o_ref):
    g = pl.program_id(1)
    mask = _head_block_mask(GRID_W)
    kc = kc_ref[...]
    vc = vc_ref[...]
    rows = GRID_W
    for i in range(NA_ROWS_PER_STEP):
        r = g * NA_ROWS_PER_STEP + i
        r0 = jnp.clip(r - NA_WIN_R // 2, 0, rows - NA_WIN_R)
        pat = jnp.where(r < 4, r, jnp.where(r > rows - 4, r - (rows - 8), 4))
        qb = _heads_on_rows(q_ref[i * GRID_W:(i + 1) * GRID_W, :], mask)
        start = pl.multiple_of(r0 * GRID_W, GRID_W)
        ks = k_ref[pl.ds(start, NA_WIN_R * GRID_W), :]
        vs = v_ref[pl.ds(start, NA_WIN_R * GRID_W), :]
        s_loc = lax.dot_general(qb, ks, _NT, preferred_element_type=F32) + bias_ref[pat]
        s_ctx = lax.dot_general(qb, kc, _NT, preferred_element_type=F32)
        m = jnp.maximum(jnp.max(s_loc, axis=-1, keepdims=True), jnp.max(s_ctx, axis=-1, keepdims=True))
        p_loc = jnp.exp(s_loc - m)
        p_ctx = jnp.exp(s_ctx - m)
        denom = jnp.sum(p_loc, axis=-1, keepdims=True) + jnp.sum(p_ctx, axis=-1, keepdims=True)
        o_all = (jnp.dot(p_loc.astype(BF16), vs, preferred_element_type=F32)
                 + jnp.dot(p_ctx.astype(BF16), vc, preferred_element_type=F32)) / denom
        o_ref[i * GRID_W:(i + 1) * GRID_W, :] = _heads_to_lanes(o_all, GRID_W, NA_HEADS, NA_HD).astype(o_ref.dtype)


def _na_latent(qkv, bias, b_, S, L):
    tq = NA_ROWS_PER_STEP * GRID_W
    steps = S // tq
    ctx_blk0 = b_ * S // L
    return pl.pallas_call(
        _na_body,
        out_shape=jax.ShapeDtypeStruct((b_ * S, NA_W), BF16),
        grid=(b_, steps),
        in_specs=[
            pl.BlockSpec((tq, NA_W), lambda b, g: (b * steps + g, 0)),
            pl.BlockSpec((S, NA_W), lambda b, g: (b, 1)),
            pl.BlockSpec((S, NA_W), lambda b, g: (b, 2)),
            pl.BlockSpec((L, NA_W), lambda b, g: (ctx_blk0 + b, 1)),
            pl.BlockSpec((L, NA_W), lambda b, g: (ctx_blk0 + b, 2)),
            pl.BlockSpec(bias.shape, lambda b, g: (0, 0, 0)),
        ],
        out_specs=pl.BlockSpec((tq, NA_W), lambda b, g: (b * steps + g, 0)),
        compiler_params=pltpu.CompilerParams(dimension_semantics=("arbitrary", "arbitrary"),
                                             vmem_limit_bytes=VMEM_LIMIT_BYTES),
        name="na_latent",
    )(qkv, qkv, qkv, qkv, qkv, bias)


def _na_ctx_body(q_ref, k_ref, v_ref, o_ref):
    n_q = q_ref.shape[0]
    qb = _heads_on_rows(q_ref[...], _head_block_mask(n_q))
    s = lax.dot_general(qb, k_ref[...], _NT, preferred_element_type=F32)
    p = jnp.exp(s - jnp.max(s, axis=-1, keepdims=True))
    o_all = jnp.dot(p.astype(BF16), v_ref[...], preferred_element_type=F32) / jnp.sum(p, axis=-1, keepdims=True)
    o_ref[...] = _heads_to_lanes(o_all, n_q, NA_HEADS, NA_HD).astype(o_ref.dtype)


def _na_context(qkv, b_, S, L):
    ctx_blk0 = b_ * S // L
    return pl.pallas_call(
        _na_ctx_body,
        out_shape=jax.ShapeDtypeStruct((b_ * L, NA_W), BF16),
        grid=(b_,),
        in_specs=[pl.BlockSpec((L, NA_W), lambda b: (ctx_blk0 + b, 0)),
                  pl.BlockSpec((L, NA_W), lambda b: (ctx_blk0 + b, 1)),
                  pl.BlockSpec((L, NA_W), lambda b: (ctx_blk0 + b, 2))],
        out_specs=pl.BlockSpec((L, NA_W), lambda b: (b, 0)),
        compiler_params=pltpu.CompilerParams(dimension_semantics=("arbitrary",)),
        name="na_context",
    )(qkv, qkv, qkv)


GLA_BLOCK = 256
GLA_SUB = 8


def _rope_lane_tables(S, L):
    half = GLA_DK // 2
    inv_freq = ROPE_BASE ** (-jnp.arange(0, half, 2, dtype=F32) / half)
    t = jnp.arange(S)
    d = np.arange(GLA_QK_W) % GLA_DK
    use_col = d >= half
    fidx = d % (half // 2)
    sign = np.where((d % half) < half // 2, -1.0, 1.0).astype(np.float32)
    pos = jnp.where(use_col[None, :], (t % GRID_W)[:, None], (t // GRID_W)[:, None]).astype(F32)
    ang = pos * inv_freq[fidx][None, :]
    cos = jnp.concatenate([jnp.cos(ang), jnp.ones((L, GLA_QK_W), F32)], 0)
    sin = jnp.concatenate([jnp.sin(ang) * sign[None, :], jnp.zeros((L, GLA_QK_W), F32)], 0)
    return cos, sin


def _rope_lanes(x, cos, sin):
    pair = GLA_DK // 4
    lane = lax.broadcasted_iota(jnp.int32, x.shape, 1)
    first = (lane % (2 * pair)) < pair
    partner = jnp.where(first, pltpu.roll(x, LANES - pair, 1), pltpu.roll(x, pair, 1))
    return x * cos + partner * sin


def _bcast_rows(x, r, n):
    return jnp.broadcast_to(x[r:r + 1, :], (n, x.shape[1]))


def _gla_consts(rev):
    C = GLA_CHUNK
    r = lax.broadcasted_iota(jnp.int32, (C, C), 0)
    c = lax.broadcasted_iota(jnp.int32, (C, C), 1)
    tri = jnp.where((c >= r) if rev else (c <= r), 1.0, 0.0).astype(F32)
    n_exp = (C // GLA_SUB - 1) * GLA_QK_W
    head_mask_q = ((lax.broadcasted_iota(jnp.int32, (GLA_HEADS * C, n_exp), 0) // C)
                   == ((lax.broadcasted_iota(jnp.int32, (GLA_HEADS * C, n_exp), 1) % GLA_QK_W) // GLA_DK))
    s3 = ((lax.broadcasted_iota(jnp.int32, (GLA_QK_W, GLA_V_W), 0) // GLA_DK)
          == (lax.broadcasted_iota(jnp.int32, (GLA_QK_W, GLA_V_W), 1) // GLA_DV)).astype(BF16)
    n_p = C * GLA_SUB
    rsum = (lax.broadcasted_iota(jnp.int32, (C, n_p), 0)
            == lax.broadcasted_iota(jnp.int32, (C, n_p), 1) // GLA_SUB).astype(BF16)
    row = lax.broadcasted_iota(jnp.int32, (n_p, GLA_QK_W), 0)
    t_loc, s_loc = (row // GLA_SUB) % GLA_SUB, row % GLA_SUB
    diag_ok = (s_loc >= t_loc) if rev else (s_loc <= t_loc)
    st_mask = ((lax.broadcasted_iota(jnp.int32, (GLA_V_W, GLA_QK_W), 0) // GLA_DV)
               == (lax.broadcasted_iota(jnp.int32, (GLA_V_W, GLA_QK_W), 1) // GLA_DK))
    return tri, head_mask_q, s3, rsum, diag_ok, st_mask


def _gla_chunk(q, k, v, la, st, rev, consts):
    tri, head_mask_q, s3, rsum, diag_ok, st_mask = consts
    C, nb = GLA_CHUNK, GLA_CHUNK // GLA_SUB
    gc = jnp.dot(tri, la, precision=_HI, preferred_element_type=F32)
    zeros = jnp.zeros((GLA_SUB, GLA_QK_W), F32)
    if not rev:
        g_tot = gc[C - 1:C]
        ref_rows = [None] + [GLA_SUB * i - 1 for i in range(1, nb)]
        blk_rows = [GLA_SUB * j + GLA_SUB - 1 for j in range(nb)]
    else:
        g_tot = gc[0:1]
        ref_rows = [GLA_SUB * (i + 1) for i in range(nb - 1)] + [None]
        blk_rows = [GLA_SUB * j for j in range(nb)]
    g_ref = jnp.concatenate([zeros if r is None else _bcast_rows(gc, r, GLA_SUB) for r in ref_rows], 0)
    g_blk = jnp.concatenate([_bcast_rows(gc, r, GLA_SUB) for r in blk_rows], 0)
    qt = q * jnp.exp(gc - g_ref)
    kh = k * jnp.exp(g_blk - gc)
    sub = lax.broadcasted_iota(jnp.int32, (C, GLA_QK_W), 0) // GLA_SUB
    k_slabs, q_slabs = [], []
    for i in range(nb):
        if ref_rows[i] is None:
            continue
        keys_ok = (sub > i) if rev else (sub < i)
        between = jnp.minimum(gc[ref_rows[i]:ref_rows[i] + 1] - g_blk, 0.0)
        k_slabs.append(jnp.where(keys_ok, kh * jnp.exp(between), 0.0))
        q_slabs.append(jnp.where(sub == i, qt, 0.0))
    k_exp = jnp.concatenate(k_slabs, 1).astype(BF16)
    q_exp = jnp.concatenate(q_slabs, 1)
    q_exp = jnp.where(head_mask_q, jnp.concatenate([q_exp] * GLA_HEADS, 0), 0.0).astype(BF16)
    attn_off = lax.dot_general(q_exp, k_exp, _NT, preferred_element_type=F32)
    vb = v.astype(BF16)
    o_off_all = jnp.dot(attn_off.astype(BF16), vb, preferred_element_type=F32)
    o = _heads_to_lanes(o_off_all, C, GLA_HEADS, GLA_DV)
    pieces = []
    for i in range(nb):
        lo = i * GLA_SUB
        k_i, g_i = k[lo:lo + GLA_SUB], gc[lo:lo + GLA_SUB]
        for t in range(GLA_SUB):
            e = jnp.exp(jnp.minimum(_bcast_rows(gc, lo + t, GLA_SUB) - g_i, 0.0))
            pieces.append(_bcast_rows(q, lo + t, GLA_SUB) * k_i * e)
    p = jnp.where(diag_ok, jnp.concatenate(pieces, 0), 0.0).astype(BF16)
    a2 = jnp.dot(p, s3, preferred_element_type=F32)
    v_rep = jnp.concatenate([v[i * GLA_SUB:(i + 1) * GLA_SUB] for i in range(nb) for _ in range(GLA_SUB)], 0)
    o = o + jnp.dot(rsum, (a2 * v_rep).astype(BF16), preferred_element_type=F32)
    q_dec = (q * jnp.exp(gc)).astype(BF16)
    k_dec = (k * jnp.exp(g_tot - gc)).astype(BF16)
    o = o + lax.dot_general(q_dec, st.astype(BF16), _NT, preferred_element_type=F32)
    upd = lax.dot_general(vb, k_dec, _TN, preferred_element_type=F32)
    return o, st * jnp.exp(g_tot) + jnp.where(st_mask, upd, 0.0)


def _gla_body(xf_ref, xb_ref, cf_ref, sf_ref, cb_ref, sb_ref, w2_ref, b2_ref, of_ref, ob_ref,
              q_s, k_s, la_s, st_s):
    @pl.when(pl.program_id(1) == 0)
    def _():
        st_s[...] = jnp.zeros_like(st_s)

    gate_lo = 2 * GLA_QK_W + 2 * GLA_V_W
    for d, (x_ref, cos_ref, sin_ref) in enumerate(((xf_ref, cf_ref, sf_ref), (xb_ref, cb_ref, sb_ref))):
        z = jnp.dot(x_ref[:, gate_lo:], w2_ref[...], precision=_HI, preferred_element_type=F32) + b2_ref[...]
        la_s[d] = jax.nn.log_sigmoid(z[:, d * GLA_QK_W:(d + 1) * GLA_QK_W]) * (1.0 / GLA_TAU)
        q_s[d] = _rope_lanes(x_ref[:, 0:GLA_QK_W], cos_ref[...], sin_ref[...]) * (GLA_DK ** -0.5)
        k_s[d] = _rope_lanes(x_ref[:, GLA_QK_W:2 * GLA_QK_W], cos_ref[...], sin_ref[...])

    consts = (_gla_consts(False), _gla_consts(True))
    n_chunks = GLA_BLOCK // GLA_CHUNK

    def step(i, carry):
        starts = (i * GLA_CHUNK, (n_chunks - 1 - i) * GLA_CHUNK)
        for d, (x_ref, o_ref) in enumerate(((xf_ref, of_ref), (xb_ref, ob_ref))):
            rows = pl.ds(pl.multiple_of(starts[d], GLA_CHUNK), GLA_CHUNK)
            v = x_ref[rows, 2 * GLA_QK_W:2 * GLA_QK_W + GLA_V_W]
            o, st = _gla_chunk(q_s[d, rows, :], k_s[d, rows, :], v, la_s[d, rows, :], st_s[d], d == 1, consts[d])
            o_ref[rows, :] = o
            st_s[d] = st
        return carry

    lax.fori_loop(0, n_chunks, step, 0)


def _gla_gate_weights(w_a2, b_a):
    w2 = jnp.zeros((LANES, 2 * GLA_QK_W), F32)
    w2 = w2.at[0:GLA_RANK, 0:GLA_QK_W].set(w_a2[0]).at[GLA_RANK:2 * GLA_RANK, GLA_QK_W:].set(w_a2[1])
    return w2, b_a.reshape(1, 2 * GLA_QK_W)


def _gla_scan(cols, cos, sin, w2, b2, b_, S, L):
    n = cols.shape[0]
    lat_blocks = S // GLA_BLOCK
    ctx0 = b_ * S // GLA_BLOCK
    assert L == GLA_BLOCK
    fwd = lambda b, j: jnp.where(j == 0, ctx0 + b, b * lat_blocks + j - 1)
    bwd = lambda b, j: jnp.where(j == 0, ctx0 + b, b * lat_blocks + lat_blocks - j)
    fwd_t = lambda b, j: jnp.where(j == 0, lat_blocks, j - 1)
    bwd_t = lambda b, j: jnp.where(j == 0, lat_blocks, lat_blocks - j)
    blk = lambda f: (lambda b, j: (f(b, j), 0))
    const = lambda b, j: (0, 0)
    return pl.pallas_call(
        _gla_body,
        out_shape=(jax.ShapeDtypeStruct((n, GLA_V_W), F32), jax.ShapeDtypeStruct((n, GLA_V_W), F32)),
        grid=(b_, lat_blocks + 1),
        in_specs=[
            pl.BlockSpec((GLA_BLOCK, GLA_SEG_W), blk(fwd)),
            pl.BlockSpec((GLA_BLOCK, GLA_SEG_W), blk(bwd)),
            pl.BlockSpec((GLA_BLOCK, GLA_QK_W), blk(fwd_t)),
            pl.BlockSpec((GLA_BLOCK, GLA_QK_W), blk(fwd_t)),
            pl.BlockSpec((GLA_BLOCK, GLA_QK_W), blk(bwd_t)),
            pl.BlockSpec((GLA_BLOCK, GLA_QK_W), blk(bwd_t)),
            pl.BlockSpec((LANES, 2 * GLA_QK_W), const),
            pl.BlockSpec((1, 2 * GLA_QK_W), const),
        ],
        out_specs=(pl.BlockSpec((GLA_BLOCK, GLA_V_W), blk(fwd)), pl.BlockSpec((GLA_BLOCK, GLA_V_W), blk(bwd))),
        scratch_shapes=[pltpu.VMEM((2, GLA_BLOCK, GLA_QK_W), F32), pltpu.VMEM((2, GLA_BLOCK, GLA_QK_W), F32),
                        pltpu.VMEM((2, GLA_BLOCK, GLA_QK_W), F32), pltpu.VMEM((2, GLA_V_W, GLA_QK_W), F32)],
        compiler_params=pltpu.CompilerParams(dimension_semantics=("arbitrary", "arbitrary"),
                                             vmem_limit_bytes=VMEM_LIMIT_BYTES),
        name="gla_scan",
    )(cols, cols, cos, sin, cos, sin, w2, b2)


def _rmsnorm(t, g):
    y = t * lax.rsqrt(jnp.mean(t * t, -1, keepdims=True) + RMS_EPS)
    return y * g


def _l2norm(t):
    return t * lax.rsqrt(jnp.sum(t * t, -1, keepdims=True) + 1e-6)


def _blocks(t, chunk):
    b_, T, H = t.shape[:3]
    return t.reshape(b_, T // chunk, chunk, H, -1).transpose(1, 0, 3, 2, 4)


def _unblocks(o):
    n, b_, H, C, d = o.shape
    return o.transpose(1, 0, 3, 2, 4).reshape(b_, n * C, H, d)


def _gdn_chunked(q, k, v, g, beta, s0):
    C = GDN_CHUNK
    dv = v.shape[-1]
    qc, kc, vc = _blocks(q, C), _blocks(k, C), _blocks(v, C)
    gc = jnp.cumsum(_blocks(g[..., None], C)[..., 0], axis=-1)
    bc = _blocks(beta[..., None], C)[..., 0]
    tril = jnp.tril(jnp.ones((C, C), dtype=bool))
    strict = jnp.tril(jnp.ones((C, C), dtype=bool), -1)
    decay = jnp.exp(jnp.where(tril, gc[..., :, None] - gc[..., None, :], -jnp.inf))
    kb = kc * bc[..., None]
    a_mat = jnp.where(strict, jnp.einsum('nbhtk,nbhsk->nbhts', kb, kc) * decay, 0.0) + jnp.eye(C, dtype=F32)
    rhs = jnp.concatenate([vc * bc[..., None], kb * jnp.exp(gc)[..., None]], axis=-1)
    sol = lax.linalg.triangular_solve(a_mat, rhs, left_side=True, lower=True)
    u, w = sol[..., :dv], sol[..., dv:]
    qk = jnp.where(tril, jnp.einsum('nbhtk,nbhsk->nbhts', qc, kc) * decay, 0.0)
    q_dec = qc * jnp.exp(gc)[..., None]
    k_dec = kc * jnp.exp(gc[..., -1:] - gc)[..., None]
    a_last = jnp.exp(gc[..., -1])

    def step(s, inp):
        qd, kd, uu, ww, qkk, al = inp
        v_new = uu - jnp.einsum('bhtk,bhkv->bhtv', ww, s)
        o = jnp.einsum('bhtk,bhkv->bhtv', qd, s) + jnp.einsum('bhts,bhsv->bhtv', qkk, v_new)
        s = al[..., None, None] * s + jnp.einsum('bhtk,bhtv->bhkv', kd, v_new)
        return s, o

    s_T, o = lax.scan(step, s0, (q_dec, k_dec, u, w, qk, a_last))
    return _unblocks(o), s_T


def _bidirectional(scan, ctx_fwd, ctx_bwd, lat_fwd, lat_bwd, s0):
    flip = lambda args: tuple(jnp.flip(a, 1) for a in args)
    o_cf, s_f = scan(*ctx_fwd, s0)
    o_cb, s_b = scan(*flip(ctx_bwd), s0)
    o_lf, _ = scan(*lat_fwd, s_f)
    o_lb, _ = scan(*flip(lat_bwd), s_b)
    return o_lf + jnp.flip(o_lb, 1), o_cf + jnp.flip(o_cb, 1)


def _dwconv_centred(t, w):
    pad = w.shape[0] // 2
    return lax.conv_general_dilated(t, w[:, None, :], window_strides=(1,), padding=[(pad, pad)],
                                    dimension_numbers=('NWC', 'WIO', 'NWC'), feature_group_count=t.shape[-1])


def _gdn_group(cols_l, cols_c, conv_w, a_log, dt_bias, norm_g):
    def prep(cols):
        b_, T = cols.shape[:2]
        qkv = jax.nn.silu(_dwconv_centred(cols[..., :2 * GDN_QK_W + GDN_V_W], conv_w))
        q, k, v = jnp.split(qkv, [GDN_QK_W, 2 * GDN_QK_W], axis=-1)
        z = cols[..., 2 * GDN_QK_W + GDN_V_W:2 * GDN_QK_W + 2 * GDN_V_W]
        lo = 2 * GDN_QK_W + 2 * GDN_V_W
        b_lr, a_lr = cols[..., lo:lo + 2 * GDN_HEADS], cols[..., lo + 2 * GDN_HEADS:lo + 4 * GDN_HEADS]
        q = _l2norm(q.reshape(b_, T, GDN_HEADS, GDN_DK)) * GDN_DK ** -0.5
        k = _l2norm(k.reshape(b_, T, GDN_HEADS, GDN_DK))
        v = v.reshape(b_, T, GDN_HEADS, GDN_DV)
        beta = jax.nn.sigmoid(b_lr).reshape(b_, T, 2, GDN_HEADS)
        g = -jnp.exp(a_log) * jax.nn.softplus(a_lr.reshape(b_, T, 2, GDN_HEADS) + dt_bias)
        return (q, k, v, g[:, :, 0], beta[:, :, 0]), (q, k, v, g[:, :, 1], beta[:, :, 1]), z

    fl, bl, z_l = prep(cols_l)
    fc, bc, z_c = prep(cols_c)
    s0 = jnp.zeros((z_l.shape[0], GDN_HEADS, GDN_DK, GDN_DV), F32)
    o_l, o_c = _bidirectional(_gdn_chunked, fc, bc, fl, bl, s0)
    out = lambda o, z: _rmsnorm(o, norm_g).reshape(z.shape) * jax.nn.silu(z)
    return out(o_l, z_l), out(o_c, z_c)


def kernel(x, c, ctx, c_ctx, w_ada, b_ada, w_in, gla_w_a2, gla_b_a, gla_norm, na_rpb, gdn_conv, gdn_a_log,
           gdn_dt_bias, gdn_norm, w_out, ln1_g, ln1_b, w_router, b_router, w_gate_up, b_gate_up, w_down, b_down,
           ln2_g, ln2_b):
    b_, S, D = x.shape
    L = ctx.shape[1]
    n_lat, n_ctx = b_ * S, b_ * L
    rope_cos, rope_sin = _rope_lane_tables(S, L)
    sc = jnp.concatenate([jax.nn.silu(c), jax.nn.silu(c_ctx)[None, :]], 0)
    xt = jnp.concatenate([x.reshape(n_lat, D), ctx.reshape(n_ctx, D)], 0)
    for l in range(DEPTH):
        last = l == DEPTH - 1
        m = (jnp.dot(sc, w_ada[l], precision=_HI) + b_ada[l]).reshape(b_ + 1, N_MOD, D)
        shift1, scale1, gate1, shift2, scale2, gate2 = (m[:, j] for j in range(N_MOD))
        gla_cols, na_cols, gdn_cols = _inproj(xt, jnp.stack([shift1, 1.0 + scale1], 1), _inproj_weights(w_in[l]), S)
        gla_w2, gla_b2 = _gla_gate_weights(gla_w_a2[l], gla_b_a[l])
        gla_f, gla_b = _gla_scan(gla_cols, rope_cos, rope_sin, gla_w2, gla_b2, b_, S, L)
        na = jnp.concatenate([_na_latent(na_cols, _na_bias_table(na_rpb[l]), b_, S, L),
                              _na_context(na_cols, b_, S, L)], 0)
        gdn_l, gdn_c = _gdn_group(gdn_cols[:n_lat].reshape(b_, S, GDN_SEG_W), gdn_cols[n_lat:].reshape(b_, L, GDN_SEG_W),
                                  gdn_conv[l], gdn_a_log[l], gdn_dt_bias[l], gdn_norm[l])
        gdn = jnp.concatenate([gdn_l.reshape(n_lat, GDN_V_W), gdn_c.reshape(n_ctx, GDN_V_W)], 0)
        n_rows = n_lat if last else n_lat + n_ctx
        w_router_pad = jnp.pad(w_router[l], ((0, 0), (0, LANES - N_EXPERTS)))
        b_router_pad = jnp.pad(b_router[l], (0, LANES - N_EXPERTS)).reshape(1, LANES)
        x1, h2, logits = _outproj(n_rows, gla_f, gla_b, gla_cols, jnp.tile(gla_norm[l], GLA_HEADS)[None, :], na, gdn, xt,
                                  jnp.stack([gate1, shift2, 1.0 + scale2], 1),
                                  jnp.stack([ln1_g[l], ln1_b[l]], 0), w_out[l].astype(BF16),
                                  w_router_pad, b_router_pad, S)
        y4, gate = _moe(l, h2, logits[:, :N_EXPERTS], w_gate_up, b_gate_up, w_down, b_down)
        xt = _combine(y4, gate, x1, gate2[:, None, :], jnp.stack([ln2_g[l], ln2_b[l]], 0), S)
    return xt[:n_lat].reshape(b_, S, D)
```

```python
import functools

import numpy as np

import jax
import jax.numpy as jnp
from jax import lax
from jax.experimental import pallas as pl
from jax.experimental.pallas import tpu as pltpu

D_MODEL = 1024
DEPTH = 4
GRID_W = 64
GLA_HEADS, GLA_DK, GLA_DV, GLA_RANK, GLA_TAU, GLA_CHUNK = 4, 32, 64, 16, 16.0, 32
ROPE_BASE = 10000.0
NA_HEADS, NA_HD, NA_WIN_R, NA_WIN_C = 4, 64, 8, 16
GDN_HEADS, GDN_DK, GDN_DV, GDN_CONV, GDN_CHUNK = 4, 128, 128, 5, 64
N_EXPERTS, TOP_K, D_EXPERT = 32, 4, 1024
SWIGLU_LIMIT, SWIGLU_ALPHA = 7.0, 1.702
N_MOD = 6
LN_EPS, RMS_EPS = 1e-5, 1e-6
DN_ALPHA = (2 * DEPTH) ** 0.25

F32 = jnp.float32
BF16 = jnp.bfloat16

LANES = 128
VMEM_LIMIT_BYTES = 56 * 1024 * 1024
ROW_TILE = 512
MOE_BM = 512
F_CHUNK = 512

GLA_QK_W = GLA_HEADS * GLA_DK
GLA_V_W = GLA_HEADS * GLA_DV
NA_W = NA_HEADS * NA_HD
GDN_QK_W = GDN_HEADS * GDN_DK
GDN_V_W = GDN_HEADS * GDN_DV
GLA_SEG_W = 2 * GLA_QK_W + 2 * GLA_V_W + LANES
NA_SEG_W = 3 * NA_W
GDN_SEG_W = 2 * GDN_QK_W + 2 * GDN_V_W + LANES
D_IN_SEG = GLA_SEG_W + NA_SEG_W + GDN_SEG_W
MIX_W = GLA_V_W + NA_W + GDN_V_W

_NT = (((1,), (1,)), ((), ()))
_TN = (((0,), (0,)), ((), ()))
_HI = lax.Precision.HIGHEST


def _ln_rows(x):
    mu = jnp.mean(x, axis=-1, keepdims=True)
    xc = x - mu
    var = jnp.mean(xc * xc, axis=-1, keepdims=True)
    return xc * lax.rsqrt(var + LN_EPS)


def _group_of_block(i, rows_per_group, n_groups):
    return jnp.minimum(i * ROW_TILE // rows_per_group, n_groups - 1)


def _inproj_weights(w):
    o = np.cumsum((0, GLA_QK_W, GLA_QK_W, GLA_V_W, 2 * GLA_RANK, GLA_V_W, NA_W, NA_W, NA_W,
                   GDN_QK_W, GDN_QK_W, GDN_V_W, 2 * GDN_HEADS, 2 * GDN_HEADS, GDN_V_W)).tolist()
    col = lambda i: w[:, o[i]:o[i + 1]]
    zeros = lambda n: jnp.zeros((w.shape[0], n), w.dtype)
    parts = [col(0), col(1), col(2), col(4), col(3), zeros(LANES - 2 * GLA_RANK),
             col(5), col(6), col(7),
             col(8), col(9), col(10), col(13), col(11), col(12), zeros(LANES - 4 * GDN_HEADS)]
    return jnp.concatenate(parts, axis=1).astype(BF16)


def _inproj_body(x_ref, mod_ref, w_ref, gla_ref, na_ref, gdn_ref):
    h = _ln_rows(x_ref[...]) * mod_ref[0, 1:2, :] + mod_ref[0, 0:1, :]
    hb = h.astype(BF16)
    gla_ref[...] = jnp.dot(hb, w_ref[:, 0:GLA_SEG_W], preferred_element_type=F32)
    na_ref[...] = jnp.dot(hb, w_ref[:, GLA_SEG_W:GLA_SEG_W + NA_SEG_W], preferred_element_type=F32).astype(BF16)
    gdn_ref[...] = jnp.dot(hb, w_ref[:, GLA_SEG_W + NA_SEG_W:], preferred_element_type=F32)


def _inproj(x, mod, w_bf, rows_per_group):
    n, d = x.shape
    n_groups = mod.shape[0]
    row = lambda i: (i, 0)
    return pl.pallas_call(
        _inproj_body,
        out_shape=(jax.ShapeDtypeStruct((n, GLA_SEG_W), F32), jax.ShapeDtypeStruct((n, NA_SEG_W), BF16),
                   jax.ShapeDtypeStruct((n, GDN_SEG_W), F32)),
        grid=(n // ROW_TILE,),
        in_specs=[
            pl.BlockSpec((ROW_TILE, d), row),
            pl.BlockSpec((1, 2, d), lambda i: (_group_of_block(i, rows_per_group, n_groups), 0, 0)),
            pl.BlockSpec((d, D_IN_SEG), lambda i: (0, 0)),
        ],
        out_specs=(pl.BlockSpec((ROW_TILE, GLA_SEG_W), row), pl.BlockSpec((ROW_TILE, NA_SEG_W), row),
                   pl.BlockSpec((ROW_TILE, GDN_SEG_W), row)),
        compiler_params=pltpu.CompilerParams(dimension_semantics=("arbitrary",),
                                             vmem_limit_bytes=VMEM_LIMIT_BYTES),
        name="inproj",
    )(x, mod, w_bf)


def _outproj_body(gf_ref, gb_ref, gg_ref, gn_ref, na_ref, df_ref, db_ref, dz_ref, dn_ref, x_ref, mod_ref, ln_ref,
                  w_ref, wr_ref, br_ref, xo_ref, h_ref, lg_ref):
    o = gf_ref[...] + gb_ref[...]
    sq = o * o
    sq_hi = sq.astype(BF16)
    sq_lo = (sq - sq_hi.astype(F32)).astype(BF16)
    head_mean = ((lax.broadcasted_iota(jnp.int32, (GLA_V_W, GLA_V_W), 0) // GLA_DV)
                 == (lax.broadcasted_iota(jnp.int32, (GLA_V_W, GLA_V_W), 1) // GLA_DV)
                 ).astype(BF16) * (1.0 / GLA_DV)
    ms = (jnp.dot(sq_hi, head_mean, preferred_element_type=F32)
          + jnp.dot(sq_lo, head_mean, preferred_element_type=F32))
    gla = o * lax.rsqrt(ms + RMS_EPS) * gn_ref[...] * jax.nn.silu(gg_ref[...])
    od = df_ref[...] + db_ref[...]
    parts = []
    for hd in range(GDN_HEADS):
        seg = od[:, hd * GDN_DV:(hd + 1) * GDN_DV]
        parts.append(seg * lax.rsqrt(jnp.mean(seg * seg, axis=-1, keepdims=True) + RMS_EPS))
    gdn = jnp.concatenate(parts, 1) * dn_ref[...] * jax.nn.silu(dz_ref[...])
    a = (jnp.dot(gla.astype(BF16), w_ref[0:GLA_V_W, :], preferred_element_type=F32)
         + jnp.dot(na_ref[...], w_ref[GLA_V_W:GLA_V_W + NA_W, :], preferred_element_type=F32)
         + jnp.dot(gdn.astype(BF16), w_ref[GLA_V_W + NA_W:, :], preferred_element_type=F32))
    y = DN_ALPHA * x_ref[...] + mod_ref[0, 0:1, :] * a
    xn = _ln_rows(y) * ln_ref[0:1, :] + ln_ref[1:2, :]
    xo_ref[...] = xn
    h = _ln_rows(xn) * mod_ref[0, 2:3, :] + mod_ref[0, 1:2, :]
    h_ref[...] = h
    lg_ref[...] = jnp.dot(h, wr_ref[...], preferred_element_type=F32, precision=_HI) + br_ref[...]


def _outproj(n, gla_f, gla_b, gla_cols, gla_norm, na, gdn_f, gdn_b, gdn_cols, gdn_norm, x, mod, ln, w_bf,
             w_router_pad, b_router_pad, rows_per_group):
    d = x.shape[1]
    n_groups = mod.shape[0]
    row = lambda i: (i, 0)
    const = lambda i: (0, 0)
    return pl.pallas_call(
        _outproj_body,
        out_shape=(jax.ShapeDtypeStruct((n, d), F32), jax.ShapeDtypeStruct((n, d), F32),
                   jax.ShapeDtypeStruct((n, LANES), F32)),
        grid=(n // ROW_TILE,),
        in_specs=[
            pl.BlockSpec((ROW_TILE, GLA_V_W), row),
            pl.BlockSpec((ROW_TILE, GLA_V_W), row),
            pl.BlockSpec((ROW_TILE, GLA_V_W), lambda i: (i, (2 * GLA_QK_W + GLA_V_W) // GLA_V_W)),
            pl.BlockSpec((1, GLA_V_W), const),
            pl.BlockSpec((ROW_TILE, NA_W), row),
            pl.BlockSpec((ROW_TILE, GDN_V_W), row),
            pl.BlockSpec((ROW_TILE, GDN_V_W), row),
            pl.BlockSpec((ROW_TILE, GDN_V_W), lambda i: (i, (2 * GDN_QK_W + GDN_V_W) // GDN_V_W)),
            pl.BlockSpec((1, GDN_V_W), const),
            pl.BlockSpec((ROW_TILE, d), row),
            pl.BlockSpec((1, 3, d), lambda i: (_group_of_block(i, rows_per_group, n_groups), 0, 0)),
            pl.BlockSpec((2, d), const),
            pl.BlockSpec((MIX_W, d), const),
            pl.BlockSpec((d, LANES), const),
            pl.BlockSpec((1, LANES), const),
        ],
        out_specs=(pl.BlockSpec((ROW_TILE, d), row), pl.BlockSpec((ROW_TILE, d), row),
                   pl.BlockSpec((ROW_TILE, LANES), row)),
        compiler_params=pltpu.CompilerParams(dimension_semantics=("arbitrary",),
                                             vmem_limit_bytes=VMEM_LIMIT_BYTES),
        name="outproj",
    )(gla_f, gla_b, gla_cols, gla_norm, na, gdn_f, gdn_b, gdn_cols, gdn_norm, x, mod, ln, w_bf, w_router_pad,
      b_router_pad)


def _moe_body(be_ref, nv_ref, x_ref, wgu_ref, bgu_ref, wd_ref, bd_ref, o_ref, wgu_bf, wd_bf):
    i = pl.program_id(0)
    e = be_ref[i]
    e_prev = be_ref[jnp.maximum(i - 1, 0)]
    valid = i < nv_ref[0]

    @pl.when(valid & ((i == 0) | (e != e_prev)))
    def _():
        wgu_bf[...] = wgu_ref[0, 0].astype(BF16)
        wd_bf[...] = wd_ref[0, 0].astype(BF16)

    @pl.when(valid)
    def _():
        x = x_ref[...].astype(BF16)
        acc = jnp.zeros((MOE_BM, D_MODEL), F32) + bd_ref[0, 0]
        for c in range(D_EXPERT // F_CHUNK):
            lo = c * F_CHUNK
            gt = jnp.dot(x, wgu_bf[:, lo:lo + F_CHUNK], preferred_element_type=F32)
            gt = gt + bgu_ref[0, 0, :, lo:lo + F_CHUNK]
            up = jnp.dot(x, wgu_bf[:, D_EXPERT + lo:D_EXPERT + lo + F_CHUNK], preferred_element_type=F32)
            up = up + bgu_ref[0, 0, :, D_EXPERT + lo:D_EXPERT + lo + F_CHUNK]
            gt = jnp.minimum(gt, SWIGLU_LIMIT)
            up = jnp.clip(up, -SWIGLU_LIMIT, SWIGLU_LIMIT)
            act = (up + 1.0) * gt * jax.nn.sigmoid(SWIGLU_ALPHA * gt)
            acc = acc + jnp.dot(act.astype(BF16), wd_bf[lo:lo + F_CHUNK, :], preferred_element_type=F32)
        o_ref[...] = acc

    @pl.when(jnp.logical_not(valid))
    def _():
        o_ref[...] = jnp.zeros_like(o_ref)


def _moe_experts(layer, block_e, n_valid, xs, w_gate_up, b_gate_up, w_down, b_down):
    n_rows, d = xs.shape
    n_blocks = n_rows // MOE_BM
    f2 = 2 * D_EXPERT

    def blk(i, be, nv):
        return jnp.minimum(i, nv[0] - 1)

    grid_spec = pltpu.PrefetchScalarGridSpec(
        num_scalar_prefetch=2,
        grid=(n_blocks,),
        in_specs=[
            pl.BlockSpec((MOE_BM, d), lambda i, be, nv: (blk(i, be, nv), 0)),
            pl.BlockSpec((1, 1, d, f2), lambda i, be, nv: (layer, be[blk(i, be, nv)], 0, 0)),
            pl.BlockSpec((1, 1, 1, f2), lambda i, be, nv: (layer, be[blk(i, be, nv)], 0, 0)),
            pl.BlockSpec((1, 1, D_EXPERT, d), lambda i, be, nv: (layer, be[blk(i, be, nv)], 0, 0)),
            pl.BlockSpec((1, 1, 1, d), lambda i, be, nv: (layer, be[blk(i, be, nv)], 0, 0)),
        ],
        out_specs=pl.BlockSpec((MOE_BM, d), lambda i, be, nv: (i, 0)),
        scratch_shapes=[pltpu.VMEM((d, f2), BF16), pltpu.VMEM((D_EXPERT, d), BF16)],
    )
    return pl.pallas_call(
        _moe_body,
        out_shape=jax.ShapeDtypeStruct((n_rows, d), F32),
        grid_spec=grid_spec,
        compiler_params=pltpu.CompilerParams(dimension_semantics=("arbitrary",),
                                             vmem_limit_bytes=VMEM_LIMIT_BYTES),
        name="moe_experts",
    )(block_e, n_valid, xs, w_gate_up, b_gate_up.reshape(DEPTH, N_EXPERTS, 1, f2), w_down,
      b_down.reshape(DEPTH, N_EXPERTS, 1, d))


GATHER_ROWS = 512


def _gather_body(idx_ref, src_ref, o_ref, sem):
    def row_copy(src_row, dst_row):
        return pltpu.make_async_copy(src_ref.at[pl.ds(src_row, 1)], o_ref.at[pl.ds(dst_row, 1)], sem.at[0])

    for r in range(GATHER_ROWS):
        row_copy(idx_ref[0, 0, r], r).start(priority=r % 2)
    for r in range(GATHER_ROWS):
        row_copy(0, r).wait()


def _gather_rows(src, idx):
    n_out = idx.shape[0]
    n_steps = n_out // GATHER_ROWS
    return pl.pallas_call(
        _gather_body,
        out_shape=jax.ShapeDtypeStruct((n_out, src.shape[1]), src.dtype),
        grid=(n_steps,),
        in_specs=[
            pl.BlockSpec((1, 1, GATHER_ROWS), lambda i: (i, 0, 0), memory_space=pltpu.SMEM),
            pl.BlockSpec(memory_space=pl.ANY),
        ],
        out_specs=pl.BlockSpec((GATHER_ROWS, src.shape[1]), lambda i: (i, 0)),
        scratch_shapes=[pltpu.SemaphoreType.DMA((1,))],
        compiler_params=pltpu.CompilerParams(dimension_semantics=("arbitrary",)),
        name="gather_rows",
    )(idx.reshape(n_steps, 1, GATHER_ROWS), src)


def _combine_body(y_ref, g_ref, x_ref, mod_ref, ln_ref, o_ref):
    d = x_ref.shape[1]
    g = g_ref[...]
    f = g[:, 0:1] * y_ref[:, 0:d]
    for k in range(1, TOP_K):
        f = f + g[:, k:k + 1] * y_ref[:, k * d:(k + 1) * d]
    y = DN_ALPHA * x_ref[...] + mod_ref[0, 0:1, :] * f
    o_ref[...] = _ln_rows(y) * ln_ref[0:1, :] + ln_ref[1:2, :]


def _combine(y4, gate, x1, gate2, ln, rows_per_group):
    n, d = x1.shape
    n_groups = gate2.shape[0]
    row = lambda i: (i, 0)
    return pl.pallas_call(
        _combine_body,
        out_shape=jax.ShapeDtypeStruct((n, d), F32),
        grid=(n // ROW_TILE,),
        in_specs=[
            pl.BlockSpec((ROW_TILE, TOP_K * d), row),
            pl.BlockSpec((ROW_TILE, TOP_K), row),
            pl.BlockSpec((ROW_TILE, d), row),
            pl.BlockSpec((1, 1, d), lambda i: (_group_of_block(i, rows_per_group, n_groups), 0, 0)),
            pl.BlockSpec((2, d), lambda i: (0, 0)),
        ],
        out_specs=pl.BlockSpec((ROW_TILE, d), row),
        compiler_params=pltpu.CompilerParams(dimension_semantics=("arbitrary",),
                                             vmem_limit_bytes=VMEM_LIMIT_BYTES),
        name="combine",
    )(y4, gate, x1, gate2, ln)


def _moe(layer, h, logits, w_gate_up, b_gate_up, w_down, b_down):
    n_tok, d = h.shape
    top_logit, top_e = lax.top_k(logits, TOP_K)
    gate = jax.nn.softmax(top_logit, axis=-1)
    nk = n_tok * TOP_K
    flat_e = top_e.reshape(-1).astype(jnp.int32)
    onehot = (flat_e[:, None] == jnp.arange(N_EXPERTS, dtype=jnp.int32)[None, :]).astype(jnp.int32)
    csum = jnp.cumsum(onehot, axis=0)
    counts = csum[-1]
    rank = jnp.sum((csum - 1) * onehot, axis=1)
    padded = (counts + MOE_BM - 1) // MOE_BM * MOE_BM
    pad_end = jnp.cumsum(padded)
    pad_start = pad_end - padded
    dest = jnp.sum(onehot * pad_start[None, :], axis=1) + rank
    n_blocks = nk // MOE_BM + N_EXPERTS
    flat_tok = jnp.arange(nk, dtype=jnp.int32) // TOP_K
    buf_tok = jnp.zeros((n_blocks * MOE_BM,), jnp.int32).at[dest].set(flat_tok)
    block_start = jnp.arange(n_blocks, dtype=jnp.int32) * MOE_BM
    block_e = jnp.minimum(jnp.searchsorted(pad_end, block_start, side='right'), N_EXPERTS - 1).astype(jnp.int32)
    n_valid = (pad_end[-1] // MOE_BM).astype(jnp.int32).reshape(1)
    xs = _gather_rows(h, buf_tok)
    yb = _moe_experts(layer, block_e, n_valid, xs, w_gate_up, b_gate_up, w_down, b_down)
    return _gather_rows(yb, dest).reshape(n_tok, TOP_K * d), gate


NA_ROWS_PER_STEP = 4
NA_PATTERN_ROWS = (0, 1, 2, 3, GRID_W // 2, GRID_W - 3, GRID_W - 2, GRID_W - 1)


def _na_bias_table(rpb):
    rows = GRID_W
    pat_r = np.array(NA_PATTERN_ROWS)
    r0 = np.clip(pat_r - NA_WIN_R // 2, 0, rows - NA_WIN_R)
    dr = r0[:, None] + np.arange(NA_WIN_R)[None, :] - pat_r[:, None]
    cidx = np.arange(GRID_W)
    c0 = np.clip(cidx - NA_WIN_C // 2, 0, GRID_W - NA_WIN_C)
    col_ok = (cidx[None, :] >= c0[:, None]) & (cidx[None, :] < c0[:, None] + NA_WIN_C)
    dc = np.clip(cidx[None, :] - cidx[:, None] + NA_WIN_C - 1, 0, 2 * NA_WIN_C - 2)
    bias = rpb[:, (dr + NA_WIN_R - 1)[:, None, :, None], dc[None, :, None, :]]
    bias = jnp.where(col_ok[None, None, :, None, :], bias, -jnp.inf)
    return bias.transpose(1, 0, 2, 3, 4).reshape(len(NA_PATTERN_ROWS), NA_HEADS * GRID_W, NA_WIN_R * GRID_W)


def _head_block_mask(n_q):
    shape = (NA_HEADS * n_q, NA_W)
    return (lax.broadcasted_iota(jnp.int32, shape, 0) // n_q) == (lax.broadcasted_iota(jnp.int32, shape, 1) // NA_HD)


def _heads_on_rows(q, mask):
    qt = jnp.concatenate([q] * NA_HEADS, axis=0)
    return jnp.where(mask, qt, jnp.zeros_like(qt)) * (NA_HD ** -0.5)


def _heads_to_lanes(o_all, n_q, n_heads, head_w):
    lane_h = lax.broadcasted_iota(jnp.int32, (n_q, n_heads * head_w), 1) // head_w
    out = jnp.zeros((n_q, n_heads * head_w), F32)
    for h in range(n_heads):
        out = out + jnp.where(lane_h == h, o_all[h * n_q:(h + 1) * n_q], 0.0)
    return out


def _na_body(q_ref, k_ref, v_ref, kc_ref, vc_ref, bias_ref, o_ref):
    g = pl.program_id(1)
    mask = _head_block_mask(GRID_W)
    kc = kc_ref[...]
    vc = vc_ref[...]
    rows = GRID_W
    for i in range(NA_ROWS_PER_STEP):
        r = g * NA_ROWS_PER_STEP + i
        r0 = jnp.clip(r - NA_WIN_R // 2, 0, rows - NA_WIN_R)
        pat = jnp.where(r < 4, r, jnp.where(r > rows - 4, r - (rows - 8), 4))
        qb = _heads_on_rows(q_ref[i * GRID_W:(i + 1) * GRID_W, :], mask)
        start = pl.multiple_of(r0 * GRID_W, GRID_W)
        ks = k_ref[pl.ds(start, NA_WIN_R * GRID_W), :]
        vs = v_ref[pl.ds(start, NA_WIN_R * GRID_W), :]
        s_loc = lax.dot_general(qb, ks, _NT, preferred_element_type=F32) + bias_ref[pat]
        s_ctx = lax.dot_general(qb, kc, _NT, preferred_element_type=F32)
        m = jnp.maximum(jnp.max(s_loc, axis=-1, keepdims=True), jnp.max(s_ctx, axis=-1, keepdims=True))
        p_loc = jnp.exp(s_loc - m)
        p_ctx = jnp.exp(s_ctx - m)
        denom = jnp.sum(p_loc, axis=-1, keepdims=True) + jnp.sum(p_ctx, axis=-1, keepdims=True)
        o_all = (jnp.dot(p_loc.astype(BF16), vs, preferred_element_type=F32)
                 + jnp.dot(p_ctx.astype(BF16), vc, preferred_element_type=F32)) / denom
        o_ref[i * GRID_W:(i + 1) * GRID_W, :] = _heads_to_lanes(o_all, GRID_W, NA_HEADS, NA_HD).astype(o_ref.dtype)


def _na_latent(qkv, bias, b_, S, L):
    tq = NA_ROWS_PER_STEP * GRID_W
    steps = S // tq
    ctx_blk0 = b_ * S // L
    return pl.pallas_call(
        _na_body,
        out_shape=jax.ShapeDtypeStruct((b_ * S, NA_W), BF16),
        grid=(b_, steps),
        in_specs=[
            pl.BlockSpec((tq, NA_W), lambda b, g: (b * steps + g, 0)),
            pl.BlockSpec((S, NA_W), lambda b, g: (b, 1)),
            pl.BlockSpec((S, NA_W), lambda b, g: (b, 2)),
            pl.BlockSpec((L, NA_W), lambda b, g: (ctx_blk0 + b, 1)),
            pl.BlockSpec((L, NA_W), lambda b, g: (ctx_blk0 + b, 2)),
            pl.BlockSpec(bias.shape, lambda b, g: (0, 0, 0)),
        ],
        out_specs=pl.BlockSpec((tq, NA_W), lambda b, g: (b * steps + g, 0)),
        compiler_params=pltpu.CompilerParams(dimension_semantics=("arbitrary", "arbitrary"),
                                             vmem_limit_bytes=VMEM_LIMIT_BYTES),
        name="na_latent",
    )(qkv, qkv, qkv, qkv, qkv, bias)


def _na_ctx_body(q_ref, k_ref, v_ref, o_ref):
    n_q = q_ref.shape[0]
    qb = _heads_on_rows(q_ref[...], _head_block_mask(n_q))
    s = lax.dot_general(qb, k_ref[...], _NT, preferred_element_type=F32)
    p = jnp.exp(s - jnp.max(s, axis=-1, keepdims=True))
    o_all = jnp.dot(p.astype(BF16), v_ref[...], preferred_element_type=F32) / jnp.sum(p, axis=-1, keepdims=True)
    o_ref[...] = _heads_to_lanes(o_all, n_q, NA_HEADS, NA_HD).astype(o_ref.dtype)


def _na_context(qkv, b_, S, L):
    ctx_blk0 = b_ * S // L
    return pl.pallas_call(
        _na_ctx_body,
        out_shape=jax.ShapeDtypeStruct((b_ * L, NA_W), BF16),
        grid=(b_,),
        in_specs=[pl.BlockSpec((L, NA_W), lambda b: (ctx_blk0 + b, 0)),
                  pl.BlockSpec((L, NA_W), lambda b: (ctx_blk0 + b, 1)),
                  pl.BlockSpec((L, NA_W), lambda b: (ctx_blk0 + b, 2))],
        out_specs=pl.BlockSpec((L, NA_W), lambda b: (b, 0)),
        compiler_params=pltpu.CompilerParams(dimension_semantics=("arbitrary",)),
        name="na_context",
    )(qkv, qkv, qkv)


GLA_BLOCK = 256
GLA_SUB = 8


def _rope_lane_tables(S, L):
    half = GLA_DK // 2
    inv_freq = ROPE_BASE ** (-jnp.arange(0, half, 2, dtype=F32) / half)
    t = jnp.arange(S)
    d = np.arange(GLA_QK_W) % GLA_DK
    use_col = d >= half
    fidx = d % (half // 2)
    sign = np.where((d % half) < half // 2, -1.0, 1.0).astype(np.float32)
    pos = jnp.where(use_col[None, :], (t % GRID_W)[:, None], (t // GRID_W)[:, None]).astype(F32)
    ang = pos * inv_freq[fidx][None, :]
    cos = jnp.concatenate([jnp.cos(ang), jnp.ones((L, GLA_QK_W), F32)], 0)
    sin = jnp.concatenate([jnp.sin(ang) * sign[None, :], jnp.zeros((L, GLA_QK_W), F32)], 0)
    return cos, sin


def _rope_lanes(x, cos, sin):
    pair = GLA_DK // 4
    lane = lax.broadcasted_iota(jnp.int32, x.shape, 1)
    first = (lane % (2 * pair)) < pair
    partner = jnp.where(first, pltpu.roll(x, LANES - pair, 1), pltpu.roll(x, pair, 1))
    return x * cos + partner * sin


def _bcast_rows(x, r, n):
    return jnp.broadcast_to(x[r:r + 1, :], (n, x.shape[1]))


def _gla_consts(rev):
    C = GLA_CHUNK
    r = lax.broadcasted_iota(jnp.int32, (C, C), 0)
    c = lax.broadcasted_iota(jnp.int32, (C, C), 1)
    tri = jnp.where((c >= r) if rev else (c <= r), 1.0, 0.0).astype(F32)
    n_exp = (C // GLA_SUB - 1) * GLA_QK_W
    head_mask_q = ((lax.broadcasted_iota(jnp.int32, (GLA_HEADS * C, n_exp), 0) // C)
                   == ((lax.broadcasted_iota(jnp.int32, (GLA_HEADS * C, n_exp), 1) % GLA_QK_W) // GLA_DK))
    s3 = ((lax.broadcasted_iota(jnp.int32, (GLA_QK_W, GLA_V_W), 0) // GLA_DK)
          == (lax.broadcasted_iota(jnp.int32, (GLA_QK_W, GLA_V_W), 1) // GLA_DV)).astype(BF16)
    n_p = C * GLA_SUB
    rsum = (lax.broadcasted_iota(jnp.int32, (C, n_p), 0)
            == lax.broadcasted_iota(jnp.int32, (C, n_p), 1) // GLA_SUB).astype(BF16)
    row = lax.broadcasted_iota(jnp.int32, (n_p, GLA_QK_W), 0)
    t_loc, s_loc = (row // GLA_SUB) % GLA_SUB, row % GLA_SUB
    diag_ok = (s_loc >= t_loc) if rev else (s_loc <= t_loc)
    st_mask = ((lax.broadcasted_iota(jnp.int32, (GLA_V_W, GLA_QK_W), 0) // GLA_DV)
               == (lax.broadcasted_iota(jnp.int32, (GLA_V_W, GLA_QK_W), 1) // GLA_DK))
    return tri, head_mask_q, s3, rsum, diag_ok, st_mask


def _gla_chunk(q, k, v, la, st, rev, consts):
    tri, head_mask_q, s3, rsum, diag_ok, st_mask = consts
    C, nb = GLA_CHUNK, GLA_CHUNK // GLA_SUB
    gc = jnp.dot(tri, la, precision=_HI, preferred_element_type=F32)
    zeros = jnp.zeros((GLA_SUB, GLA_QK_W), F32)
    if not rev:
        g_tot = gc[C - 1:C]
        ref_rows = [None] + [GLA_SUB * i - 1 for i in range(1, nb)]
        blk_rows = [GLA_SUB * j + GLA_SUB - 1 for j in range(nb)]
    else:
        g_tot = gc[0:1]
        ref_rows = [GLA_SUB * (i + 1) for i in range(nb - 1)] + [None]
        blk_rows = [GLA_SUB * j for j in range(nb)]
    g_ref = jnp.concatenate([zeros if r is None else _bcast_rows(gc, r, GLA_SUB) for r in ref_rows], 0)
    g_blk = jnp.concatenate([_bcast_rows(gc, r, GLA_SUB) for r in blk_rows], 0)
    qt = q * jnp.exp(gc - g_ref)
    kh = k * jnp.exp(g_blk - gc)
    sub = lax.broadcasted_iota(jnp.int32, (C, GLA_QK_W), 0) // GLA_SUB
    k_slabs, q_slabs = [], []
    for i in range(nb):
        if ref_rows[i] is None:
            continue
        keys_ok = (sub > i) if rev else (sub < i)
        between = jnp.minimum(gc[ref_rows[i]:ref_rows[i] + 1] - g_blk, 0.0)
        k_slabs.append(jnp.where(keys_ok, kh * jnp.exp(between), 0.0))
        q_slabs.append(jnp.where(sub == i, qt, 0.0))
    k_exp = jnp.concatenate(k_slabs, 1).astype(BF16)
    q_exp = jnp.concatenate(q_slabs, 1)
    q_exp = jnp.where(head_mask_q, jnp.concatenate([q_exp] * GLA_HEADS, 0), 0.0).astype(BF16)
    attn_off = lax.dot_general(q_exp, k_exp, _NT, preferred_element_type=F32)
    vb = v.astype(BF16)
    o_off_all = jnp.dot(attn_off.astype(BF16), vb, preferred_element_type=F32)
    o = _heads_to_lanes(o_off_all, C, GLA_HEADS, GLA_DV)
    pieces = []
    for i in range(nb):
        lo = i * GLA_SUB
        k_i, g_i = k[lo:lo + GLA_SUB], gc[lo:lo + GLA_SUB]
        for t in range(GLA_SUB):
            e = jnp.exp(jnp.minimum(_bcast_rows(gc, lo + t, GLA_SUB) - g_i, 0.0))
            pieces.append(_bcast_rows(q, lo + t, GLA_SUB) * k_i * e)
    p = jnp.where(diag_ok, jnp.concatenate(pieces, 0), 0.0).astype(BF16)
    a2 = jnp.dot(p, s3, preferred_element_type=F32)
    v_rep = jnp.concatenate([v[i * GLA_SUB:(i + 1) * GLA_SUB] for i in range(nb) for _ in range(GLA_SUB)], 0)
    o = o + jnp.dot(rsum, (a2 * v_rep).astype(BF16), preferred_element_type=F32)
    q_dec = (q * jnp.exp(gc)).astype(BF16)
    k_dec = (k * jnp.exp(g_tot - gc)).astype(BF16)
    o = o + lax.dot_general(q_dec, st.astype(BF16), _NT, preferred_element_type=F32)
    upd = lax.dot_general(vb, k_dec, _TN, preferred_element_type=F32)
    return o, st * jnp.exp(g_tot) + jnp.where(st_mask, upd, 0.0)


def _gla_body(xf_ref, xb_ref, cf_ref, sf_ref, cb_ref, sb_ref, w2_ref, b2_ref, of_ref, ob_ref,
              q_s, k_s, la_s, st_s):
    @pl.when(pl.program_id(1) == 0)
    def _():
        st_s[...] = jnp.zeros_like(st_s)

    gate_lo = 2 * GLA_QK_W + 2 * GLA_V_W
    for d, (x_ref, cos_ref, sin_ref) in enumerate(((xf_ref, cf_ref, sf_ref), (xb_ref, cb_ref, sb_ref))):
        z = jnp.dot(x_ref[:, gate_lo:], w2_ref[...], precision=_HI, preferred_element_type=F32) + b2_ref[...]
        la_s[d] = jax.nn.log_sigmoid(z[:, d * GLA_QK_W:(d + 1) * GLA_QK_W]) * (1.0 / GLA_TAU)
        q_s[d] = _rope_lanes(x_ref[:, 0:GLA_QK_W], cos_ref[...], sin_ref[...]) * (GLA_DK ** -0.5)
        k_s[d] = _rope_lanes(x_ref[:, GLA_QK_W:2 * GLA_QK_W], cos_ref[...], sin_ref[...])

    consts = (_gla_consts(False), _gla_consts(True))
    n_chunks = GLA_BLOCK // GLA_CHUNK

    def step(i, carry):
        starts = (i * GLA_CHUNK, (n_chunks - 1 - i) * GLA_CHUNK)
        for d, (x_ref, o_ref) in enumerate(((xf_ref, of_ref), (xb_ref, ob_ref))):
            rows = pl.ds(pl.multiple_of(starts[d], GLA_CHUNK), GLA_CHUNK)
            v = x_ref[rows, 2 * GLA_QK_W:2 * GLA_QK_W + GLA_V_W]
            o, st = _gla_chunk(q_s[d, rows, :], k_s[d, rows, :], v, la_s[d, rows, :], st_s[d], d == 1, consts[d])
            o_ref[rows, :] = o
            st_s[d] = st
        return carry

    lax.fori_loop(0, n_chunks, step, 0)


def _gla_gate_weights(w_a2, b_a):
    w2 = jnp.zeros((LANES, 2 * GLA_QK_W), F32)
    w2 = w2.at[0:GLA_RANK, 0:GLA_QK_W].set(w_a2[0]).at[GLA_RANK:2 * GLA_RANK, GLA_QK_W:].set(w_a2[1])
    return w2, b_a.reshape(1, 2 * GLA_QK_W)


def _gla_scan(cols, cos, sin, w2, b2, b_, S, L):
    n = cols.shape[0]
    lat_blocks = S // GLA_BLOCK
    ctx0 = b_ * S // GLA_BLOCK
    assert L == GLA_BLOCK
    fwd = lambda b, j: jnp.where(j == 0, ctx0 + b, b * lat_blocks + j - 1)
    bwd = lambda b, j: jnp.where(j == 0, ctx0 + b, b * lat_blocks + lat_blocks - j)
    fwd_t = lambda b, j: jnp.where(j == 0, lat_blocks, j - 1)
    bwd_t = lambda b, j: jnp.where(j == 0, lat_blocks, lat_blocks - j)
    blk = lambda f: (lambda b, j: (f(b, j), 0))
    const = lambda b, j: (0, 0)
    return pl.pallas_call(
        _gla_body,
        out_shape=(jax.ShapeDtypeStruct((n, GLA_V_W), F32), jax.ShapeDtypeStruct((n, GLA_V_W), F32)),
        grid=(b_, lat_blocks + 1),
        in_specs=[
            pl.BlockSpec((GLA_BLOCK, GLA_SEG_W), blk(fwd)),
            pl.BlockSpec((GLA_BLOCK, GLA_SEG_W), blk(bwd)),
            pl.BlockSpec((GLA_BLOCK, GLA_QK_W), blk(fwd_t)),
            pl.BlockSpec((GLA_BLOCK, GLA_QK_W), blk(fwd_t)),
            pl.BlockSpec((GLA_BLOCK, GLA_QK_W), blk(bwd_t)),
            pl.BlockSpec((GLA_BLOCK, GLA_QK_W), blk(bwd_t)),
            pl.BlockSpec((LANES, 2 * GLA_QK_W), const),
            pl.BlockSpec((1, 2 * GLA_QK_W), const),
        ],
        out_specs=(pl.BlockSpec((GLA_BLOCK, GLA_V_W), blk(fwd)), pl.BlockSpec((GLA_BLOCK, GLA_V_W), blk(bwd))),
        scratch_shapes=[pltpu.VMEM((2, GLA_BLOCK, GLA_QK_W), F32), pltpu.VMEM((2, GLA_BLOCK, GLA_QK_W), F32),
                        pltpu.VMEM((2, GLA_BLOCK, GLA_QK_W), F32), pltpu.VMEM((2, GLA_V_W, GLA_QK_W), F32)],
        compiler_params=pltpu.CompilerParams(dimension_semantics=("arbitrary", "arbitrary"),
                                             vmem_limit_bytes=VMEM_LIMIT_BYTES),
        name="gla_scan",
    )(cols, cols, cos, sin, cos, sin, w2, b2)


SUBLANES = 8
GDN_CONV_W = 2 * GDN_QK_W + GDN_V_W
GDN_GATE_LO = 2 * GDN_QK_W + 2 * GDN_V_W
GDN_BLOCK = 256
GDN_HT = GDN_HEADS * GDN_CHUNK


def _stack_heads(x):
    return jnp.concatenate([x[:, h * GDN_DK:(h + 1) * GDN_DK] for h in range(GDN_HEADS)], 0)


def _stack_cols(a, lane0, width):
    return jnp.concatenate([jnp.broadcast_to(a[:, lane0 + h:lane0 + h + 1], (a.shape[0], width))
                            for h in range(GDN_HEADS)], 0)


def _block_diag(xs, mask):
    return jnp.where(mask, jnp.concatenate([xs] * GDN_HEADS, 1), 0.0).astype(BF16)


def _gdn_consts(rev):
    C, HT = GDN_CHUNK, GDN_HT
    r = lax.broadcasted_iota(jnp.int32, (C, C), 0)
    c = lax.broadcasted_iota(jnp.int32, (C, C), 1)
    tri = jnp.where((c >= r) if rev else (c <= r), 1.0, 0.0).astype(F32)
    rr = lax.broadcasted_iota(jnp.int32, (HT, HT), 0)
    cc = lax.broadcasted_iota(jnp.int32, (HT, HT), 1)
    same_head = (rr // C) == (cc // C)
    t, s = rr % C, cc % C
    strict = same_head & ((s > t) if rev else (s < t))
    incl = same_head & ((s >= t) if rev else (s <= t))
    bd_mask = ((lax.broadcasted_iota(jnp.int32, (HT, GDN_QK_W), 0) // C)
               == (lax.broadcasted_iota(jnp.int32, (HT, GDN_QK_W), 1) // GDN_DK))
    pick_row0 = (lax.broadcasted_iota(jnp.int32, (SUBLANES, HT), 1) == 0).astype(F32)
    return tri, strict, incl, bd_mask, pick_row0


def _gdn_chunk(qs, ks, vs, gcol, bcol, st, rev, consts):
    tri, strict, incl, bd_mask, pick_row0 = consts
    C = GDN_CHUNK
    g_row = lax.dot_general(pick_row0, gcol, _NT, precision=_HI, preferred_element_type=F32)[0:1]
    dec = jnp.exp(jnp.minimum(gcol - g_row, 0.0))
    kb = ks * bcol
    k_bd = _block_diag(ks, bd_mask)
    n = jnp.where(strict, lax.dot_general(_block_diag(kb, bd_mask), k_bd, _NT, preferred_element_type=F32) * dec, 0.0)
    qk = jnp.where(incl, lax.dot_general(_block_diag(qs, bd_mask), k_bd, _NT, preferred_element_type=F32) * dec, 0.0)
    tp = -n
    p = jnp.dot(n.astype(BF16), n.astype(BF16), preferred_element_type=F32)
    n_factors = int(np.log2(C)) - 1
    for it in range(n_factors):
        pb = p.astype(BF16)
        tp = tp + p + jnp.dot(tp.astype(BF16), pb, preferred_element_type=F32)
        if it < n_factors - 1:
            p = jnp.dot(pb, pb, preferred_element_type=F32)
    g = gcol[:, :GDN_DK]
    last = 0 if rev else C - 1
    g_last = jnp.concatenate([_bcast_rows(g, h * C + last, C) for h in range(GDN_HEADS)], 0)
    eg = jnp.exp(g)
    q_dec = qs * eg
    kbe = kb * eg
    k_dec = ks * jnp.exp(g_last - g)
    stb = st.astype(BF16)
    r = vs * bcol - jnp.dot(_block_diag(kbe, bd_mask), stb, preferred_element_type=F32)
    v_new = r + jnp.dot(tp.astype(BF16), r.astype(BF16), preferred_element_type=F32)
    vnb = v_new.astype(BF16)
    o = (jnp.dot(_block_diag(q_dec, bd_mask), stb, preferred_element_type=F32)
         + jnp.dot(qk.astype(BF16), vnb, preferred_element_type=F32))
    a_last = jnp.exp(jnp.concatenate([g_last[h * C:(h + 1) * C] for h in range(GDN_HEADS)
                                      for _ in range(GDN_DK // C)], 0))
    st_new = st * a_last + lax.dot_general(_block_diag(k_dec, bd_mask), vnb, _TN, preferred_element_type=F32)
    return o, st_new


def _gdn_prep(x_ref, prev_ref, next_ref, has_prev, has_next, w_ref, nega_ref, dt_ref):
    pad = GDN_CONV // 2
    cur = x_ref[:, 0:GDN_CONV_W]
    before = jnp.where(has_prev, prev_ref[SUBLANES - pad:SUBLANES, 0:GDN_CONV_W], 0.0)
    after = jnp.where(has_next, next_ref[0:pad, 0:GDN_CONV_W], 0.0)
    xe = jnp.concatenate([before, cur, after], 0)
    y = w_ref[0:1, :] * xe[0:GDN_BLOCK]
    for j in range(1, GDN_CONV):
        y = y + w_ref[j:j + 1, :] * xe[j:j + GDN_BLOCK]
    y = y * jax.nn.sigmoid(y)

    def l2n(t):
        parts = []
        for h in range(GDN_HEADS):
            seg = t[:, h * GDN_DK:(h + 1) * GDN_DK]
            parts.append(seg * lax.rsqrt(jnp.sum(seg * seg, -1, keepdims=True) + 1e-6))
        return jnp.concatenate(parts, 1)

    q = l2n(y[:, 0:GDN_QK_W]) * (GDN_DK ** -0.5)
    k = l2n(y[:, GDN_QK_W:2 * GDN_QK_W])
    v = y[:, 2 * GDN_QK_W:]
    tail = x_ref[:, GDN_GATE_LO:]
    beta = jax.nn.sigmoid(tail)
    tb = tail + dt_ref[...]
    softplus = jnp.maximum(tb, 0.0) + jnp.log(1.0 + jnp.exp(-jnp.abs(tb)))
    return q, k, v, beta, nega_ref[...] * softplus


def _gdn_body(lat_blocks, xf_ref, pf_ref, nf_ref, xb_ref, pb_ref, nb_ref, w_ref, nega_ref, dt_ref, of_ref, ob_ref,
              q_s, k_s, v_s, b_s, g_s, st_s):
    j = pl.program_id(1)

    @pl.when(j == 0)
    def _():
        st_s[...] = jnp.zeros_like(st_s)

    t_blk = (j - 1, lat_blocks - j)
    for d, (x_ref, p_ref, n_ref) in enumerate(((xf_ref, pf_ref, nf_ref), (xb_ref, pb_ref, nb_ref))):
        has_prev = (j > 0) & (t_blk[d] > 0)
        has_next = (j > 0) & (t_blk[d] < lat_blocks - 1)
        q_s[d], k_s[d], v_s[d], b_s[d], g_s[d] = _gdn_prep(x_ref, p_ref, n_ref, has_prev, has_next,
                                                           w_ref, nega_ref, dt_ref)

    consts = (_gdn_consts(False), _gdn_consts(True))
    n_chunks = GDN_BLOCK // GDN_CHUNK

    def step(i, carry):
        starts = (i * GDN_CHUNK, (n_chunks - 1 - i) * GDN_CHUNK)
        for d, o_ref in enumerate((of_ref, ob_ref)):
            rows = pl.ds(pl.multiple_of(starts[d], GDN_CHUNK), GDN_CHUNK)
            gc = jnp.dot(consts[d][0], g_s[d, rows, :], precision=_HI, preferred_element_type=F32)
            gcol = _stack_cols(gc, (2 + d) * GDN_HEADS, GDN_HT)
            bcol = _stack_cols(b_s[d, rows, :], d * GDN_HEADS, GDN_DK)
            o, st = _gdn_chunk(_stack_heads(q_s[d, rows, :]), _stack_heads(k_s[d, rows, :]),
                               _stack_heads(v_s[d, rows, :]), gcol, bcol, st_s[d], d == 1, consts[d])
            o_ref[rows, :] = jnp.concatenate([o[h * GDN_CHUNK:(h + 1) * GDN_CHUNK] for h in range(GDN_HEADS)], 1)
            st_s[d] = st
        return carry

    lax.fori_loop(0, n_chunks, step, 0)


def _gdn_gate_consts(a_log, dt_bias):
    nega = jnp.zeros((1, LANES), F32).at[0, 2 * GDN_HEADS:4 * GDN_HEADS].set(-jnp.exp(a_log).reshape(-1))
    dt = jnp.zeros((1, LANES), F32).at[0, 2 * GDN_HEADS:4 * GDN_HEADS].set(dt_bias.reshape(-1))
    return nega, dt


def _gdn_scan(cols, conv_w, nega, dt, b_, S, L):
    n = cols.shape[0]
    lat_blocks = S // GDN_BLOCK
    ctx0 = b_ * S // GDN_BLOCK
    assert L == GDN_BLOCK
    per8 = GDN_BLOCK // SUBLANES
    n8 = n // SUBLANES
    fwd = lambda b, j: jnp.where(j == 0, ctx0 + b, b * lat_blocks + j - 1)
    bwd = lambda b, j: jnp.where(j == 0, ctx0 + b, b * lat_blocks + lat_blocks - j)
    blk = lambda f: (lambda b, j: (f(b, j), 0))
    prev8 = lambda f: (lambda b, j: (jnp.maximum(f(b, j) * per8 - 1, 0), 0))
    next8 = lambda f: (lambda b, j: (jnp.minimum(f(b, j) * per8 + per8, n8 - 1), 0))
    const = lambda b, j: (0, 0)
    x_spec = lambda f: pl.BlockSpec((GDN_BLOCK, GDN_SEG_W), blk(f))
    halo = lambda m: pl.BlockSpec((SUBLANES, GDN_SEG_W), m)
    return pl.pallas_call(
        functools.partial(_gdn_body, lat_blocks),
        out_shape=(jax.ShapeDtypeStruct((n, GDN_V_W), F32), jax.ShapeDtypeStruct((n, GDN_V_W), F32)),
        grid=(b_, lat_blocks + 1),
        in_specs=[x_spec(fwd), halo(prev8(fwd)), halo(next8(fwd)), x_spec(bwd), halo(prev8(bwd)), halo(next8(bwd)),
                  pl.BlockSpec((GDN_CONV, GDN_CONV_W), const), pl.BlockSpec((1, LANES), const),
                  pl.BlockSpec((1, LANES), const)],
        out_specs=(pl.BlockSpec((GDN_BLOCK, GDN_V_W), blk(fwd)), pl.BlockSpec((GDN_BLOCK, GDN_V_W), blk(bwd))),
        scratch_shapes=[pltpu.VMEM((2, GDN_BLOCK, GDN_QK_W), F32), pltpu.VMEM((2, GDN_BLOCK, GDN_QK_W), F32),
                        pltpu.VMEM((2, GDN_BLOCK, GDN_V_W), F32), pltpu.VMEM((2, GDN_BLOCK, LANES), F32),
                        pltpu.VMEM((2, GDN_BLOCK, LANES), F32), pltpu.VMEM((2, GDN_QK_W, GDN_DV), F32)],
        compiler_params=pltpu.CompilerParams(dimension_semantics=("arbitrary", "arbitrary"),
                                             vmem_limit_bytes=VMEM_LIMIT_BYTES),
        name="gdn_scan",
    )(cols, cols, cols, cols, cols, cols, conv_w, nega, dt)


def kernel(x, c, ctx, c_ctx, w_ada, b_ada, w_in, gla_w_a2, gla_b_a, gla_norm, na_rpb, gdn_conv, gdn_a_log,
           gdn_dt_bias, gdn_norm, w_out, ln1_g, ln1_b, w_router, b_router, w_gate_up, b_gate_up, w_down, b_down,
           ln2_g, ln2_b):
    b_, S, D = x.shape
    L = ctx.shape[1]
    n_lat, n_ctx = b_ * S, b_ * L
    rope_cos, rope_sin = _rope_lane_tables(S, L)
    sc = jnp.concatenate([jax.nn.silu(c), jax.nn.silu(c_ctx)[None, :]], 0)
    xt = jnp.concatenate([x.reshape(n_lat, D), ctx.reshape(n_ctx, D)], 0)
    for l in range(DEPTH):
        last = l == DEPTH - 1
        m = (jnp.dot(sc, w_ada[l], precision=_HI) + b_ada[l]).reshape(b_ + 1, N_MOD, D)
        shift1, scale1, gate1, shift2, scale2, gate2 = (m[:, j] for j in range(N_MOD))
        gla_cols, na_cols, gdn_cols = _inproj(xt, jnp.stack([shift1, 1.0 + scale1], 1), _inproj_weights(w_in[l]), S)
        gla_w2, gla_b2 = _gla_gate_weights(gla_w_a2[l], gla_b_a[l])
        gla_f, gla_b = _gla_scan(gla_cols, rope_cos, rope_sin, gla_w2, gla_b2, b_, S, L)
        na = jnp.concatenate([_na_latent(na_cols, _na_bias_table(na_rpb[l]), b_, S, L),
                              _na_context(na_cols, b_, S, L)], 0)
        gdn_nega, gdn_dt = _gdn_gate_consts(gdn_a_log[l], gdn_dt_bias[l])
        gdn_f, gdn_b = _gdn_scan(gdn_cols, gdn_conv[l], gdn_nega, gdn_dt, b_, S, L)
        n_rows = n_lat if last else n_lat + n_ctx
        w_router_pad = jnp.pad(w_router[l], ((0, 0), (0, LANES - N_EXPERTS)))
        b_router_pad = jnp.pad(b_router[l], (0, LANES - N_EXPERTS)).reshape(1, LANES)
        x1, h2, logits = _outproj(n_rows, gla_f, gla_b, gla_cols, jnp.tile(gla_norm[l], GLA_HEADS)[None, :], na,
                                  gdn_f, gdn_b, gdn_cols, jnp.tile(gdn_norm[l], GDN_HEADS)[None, :], xt,
                                  jnp.stack([gate1, shift2, 1.0 + scale2], 1),
                                  jnp.stack([ln1_g[l], ln1_b[l]], 0), w_out[l].astype(BF16),
                                  w_router_pad, b_router_pad, S)
        y4, gate = _moe(l, h2, logits[:, :N_EXPERTS], w_gate_up, b_gate_up, w_down, b_down)
        xt = _combine(y4, gate, x1, gate2[:, None, :], jnp.stack([ln2_g[l], ln2_b[l]], 0), S)
    return xt[:n_lat].reshape(b_, S, D)
```

```python
import functools

import numpy as np

import jax
import jax.numpy as jnp
from jax import lax
from jax.experimental import pallas as pl
from jax.experimental.pallas import tpu as pltpu

D_MODEL = 1024
DEPTH = 4
GRID_W = 64
GLA_HEADS, GLA_DK, GLA_DV, GLA_RANK, GLA_TAU, GLA_CHUNK = 4, 32, 64, 16, 16.0, 32
ROPE_BASE = 10000.0
NA_HEADS, NA_HD, NA_WIN_R, NA_WIN_C = 4, 64, 8, 16
GDN_HEADS, GDN_DK, GDN_DV, GDN_CONV, GDN_CHUNK = 4, 128, 128, 5, 64
N_EXPERTS, TOP_K, D_EXPERT = 32, 4, 1024
SWIGLU_LIMIT, SWIGLU_ALPHA = 7.0, 1.702
N_MOD = 6
LN_EPS, RMS_EPS = 1e-5, 1e-6
DN_ALPHA = (2 * DEPTH) ** 0.25

F32 = jnp.float32
BF16 = jnp.bfloat16

LANES = 128
VMEM_LIMIT_BYTES = 56 * 1024 * 1024
ROW_TILE = 512
MOE_BM = 512
F_CHUNK = 512

GLA_QK_W = GLA_HEADS * GLA_DK
GLA_V_W = GLA_HEADS * GLA_DV
NA_W = NA_HEADS * NA_HD
GDN_QK_W = GDN_HEADS * GDN_DK
GDN_V_W = GDN_HEADS * GDN_DV
GLA_SEG_W = 2 * GLA_QK_W + 2 * GLA_V_W + LANES
NA_SEG_W = 3 * NA_W
GDN_SEG_W = 2 * GDN_QK_W + 2 * GDN_V_W + LANES
D_IN_SEG = GLA_SEG_W + NA_SEG_W + GDN_SEG_W
MIX_W = GLA_V_W + NA_W + GDN_V_W

_NT = (((1,), (1,)), ((), ()))
_TN = (((0,), (0,)), ((), ()))
_HI = lax.Precision.HIGHEST


def _ln_rows(x):
    mu = jnp.mean(x, axis=-1, keepdims=True)
    xc = x - mu
    var = jnp.mean(xc * xc, axis=-1, keepdims=True)
    return xc * lax.rsqrt(var + LN_EPS)


def _group_of_block(i, rows_per_group, n_groups):
    return jnp.minimum(i * ROW_TILE // rows_per_group, n_groups - 1)


def _inproj_weights(w):
    o = np.cumsum((0, GLA_QK_W, GLA_QK_W, GLA_V_W, 2 * GLA_RANK, GLA_V_W, NA_W, NA_W, NA_W,
                   GDN_QK_W, GDN_QK_W, GDN_V_W, 2 * GDN_HEADS, 2 * GDN_HEADS, GDN_V_W)).tolist()
    col = lambda i: w[:, o[i]:o[i + 1]]
    zeros = lambda n: jnp.zeros((w.shape[0], n), w.dtype)
    parts = [col(0), col(1), col(2), col(4), col(3), zeros(LANES - 2 * GLA_RANK),
             col(5), col(6), col(7),
             col(8), col(9), col(10), col(13), col(11), col(12), zeros(LANES - 4 * GDN_HEADS)]
    return jnp.concatenate(parts, axis=1).astype(BF16)


def _inproj_body(x_ref, mod_ref, w_ref, gla_ref, na_ref, gdn_ref):
    h = _ln_rows(x_ref[...]) * mod_ref[0, 1:2, :] + mod_ref[0, 0:1, :]
    hb = h.astype(BF16)
    gla_ref[...] = jnp.dot(hb, w_ref[:, 0:GLA_SEG_W], preferred_element_type=F32)
    na_ref[...] = jnp.dot(hb, w_ref[:, GLA_SEG_W:GLA_SEG_W + NA_SEG_W], preferred_element_type=F32).astype(BF16)
    gdn_ref[...] = jnp.dot(hb, w_ref[:, GLA_SEG_W + NA_SEG_W:], preferred_element_type=F32)


def _inproj(x, mod, w_bf, rows_per_group):
    n, d = x.shape
    n_groups = mod.shape[0]
    row = lambda i: (i, 0)
    return pl.pallas_call(
        _inproj_body,
        out_shape=(jax.ShapeDtypeStruct((n, GLA_SEG_W), F32), jax.ShapeDtypeStruct((n, NA_SEG_W), BF16),
                   jax.ShapeDtypeStruct((n, GDN_SEG_W), F32)),
        grid=(n // ROW_TILE,),
        in_specs=[
            pl.BlockSpec((ROW_TILE, d), row),
            pl.BlockSpec((1, 2, d), lambda i: (_group_of_block(i, rows_per_group, n_groups), 0, 0)),
            pl.BlockSpec((d, D_IN_SEG), lambda i: (0, 0)),
        ],
        out_specs=(pl.BlockSpec((ROW_TILE, GLA_SEG_W), row), pl.BlockSpec((ROW_TILE, NA_SEG_W), row),
                   pl.BlockSpec((ROW_TILE, GDN_SEG_W), row)),
        compiler_params=pltpu.CompilerParams(dimension_semantics=("arbitrary",),
                                             vmem_limit_bytes=VMEM_LIMIT_BYTES),
        name="inproj",
    )(x, mod, w_bf)


def _outproj_body(gf_ref, gb_ref, gg_ref, gn_ref, na_ref, df_ref, db_ref, dz_ref, dn_ref, x_ref, mod_ref, ln_ref,
                  w_ref, wr_ref, br_ref, xo_ref, h_ref, lg_ref):
    o = gf_ref[...] + gb_ref[...]
    sq = o * o
    sq_hi = sq.astype(BF16)
    sq_lo = (sq - sq_hi.astype(F32)).astype(BF16)
    head_mean = ((lax.broadcasted_iota(jnp.int32, (GLA_V_W, GLA_V_W), 0) // GLA_DV)
                 == (lax.broadcasted_iota(jnp.int32, (GLA_V_W, GLA_V_W), 1) // GLA_DV)
                 ).astype(BF16) * (1.0 / GLA_DV)
    ms = (jnp.dot(sq_hi, head_mean, preferred_element_type=F32)
          + jnp.dot(sq_lo, head_mean, preferred_element_type=F32))
    gla = o * lax.rsqrt(ms + RMS_EPS) * gn_ref[...] * jax.nn.silu(gg_ref[...])
    od = df_ref[...] + db_ref[...]
    parts = []
    for hd in range(GDN_HEADS):
        seg = od[:, hd * GDN_DV:(hd + 1) * GDN_DV]
        parts.append(seg * lax.rsqrt(jnp.mean(seg * seg, axis=-1, keepdims=True) + RMS_EPS))
    gdn = jnp.concatenate(parts, 1) * dn_ref[...] * jax.nn.silu(dz_ref[...])
    a = (jnp.dot(gla.astype(BF16), w_ref[0:GLA_V_W, :], preferred_element_type=F32)
         + jnp.dot(na_ref[...], w_ref[GLA_V_W:GLA_V_W + NA_W, :], preferred_element_type=F32)
         + jnp.dot(gdn.astype(BF16), w_ref[GLA_V_W + NA_W:, :], preferred_element_type=F32))
    y = DN_ALPHA * x_ref[...] + mod_ref[0, 0:1, :] * a
    xn = _ln_rows(y) * ln_ref[0:1, :] + ln_ref[1:2, :]
    xo_ref[...] = xn
    h = _ln_rows(xn) * mod_ref[0, 2:3, :] + mod_ref[0, 1:2, :]
    h_ref[...] = h
    lg_ref[...] = jnp.dot(h, wr_ref[...], preferred_element_type=F32, precision=_HI) + br_ref[...]


def _outproj(n, gla_f, gla_b, gla_cols, gla_norm, na, gdn_f, gdn_b, gdn_cols, gdn_norm, x, mod, ln, w_bf,
             w_router_pad, b_router_pad, rows_per_group):
    d = x.shape[1]
    n_groups = mod.shape[0]
    row = lambda i: (i, 0)
    const = lambda i: (0, 0)
    return pl.pallas_call(
        _outproj_body,
        out_shape=(jax.ShapeDtypeStruct((n, d), F32), jax.ShapeDtypeStruct((n, d), F32),
                   jax.ShapeDtypeStruct((n, LANES), F32)),
        grid=(n // ROW_TILE,),
        in_specs=[
            pl.BlockSpec((ROW_TILE, GLA_V_W), row),
            pl.BlockSpec((ROW_TILE, GLA_V_W), row),
            pl.BlockSpec((ROW_TILE, GLA_V_W), lambda i: (i, (2 * GLA_QK_W + GLA_V_W) // GLA_V_W)),
            pl.BlockSpec((1, GLA_V_W), const),
            pl.BlockSpec((ROW_TILE, NA_W), row),
            pl.BlockSpec((ROW_TILE, GDN_V_W), row),
            pl.BlockSpec((ROW_TILE, GDN_V_W), row),
            pl.BlockSpec((ROW_TILE, GDN_V_W), lambda i: (i, (2 * GDN_QK_W + GDN_V_W) // GDN_V_W)),
            pl.BlockSpec((1, GDN_V_W), const),
            pl.BlockSpec((ROW_TILE, d), row),
            pl.BlockSpec((1, 3, d), lambda i: (_group_of_block(i, rows_per_group, n_groups), 0, 0)),
            pl.BlockSpec((2, d), const),
            pl.BlockSpec((MIX_W, d), const),
            pl.BlockSpec((d, LANES), const),
            pl.BlockSpec((1, LANES), const),
        ],
        out_specs=(pl.BlockSpec((ROW_TILE, d), row), pl.BlockSpec((ROW_TILE, d), row),
                   pl.BlockSpec((ROW_TILE, LANES), row)),
        compiler_params=pltpu.CompilerParams(dimension_semantics=("arbitrary",),
                                             vmem_limit_bytes=VMEM_LIMIT_BYTES),
        name="outproj",
    )(gla_f, gla_b, gla_cols, gla_norm, na, gdn_f, gdn_b, gdn_cols, gdn_norm, x, mod, ln, w_bf, w_router_pad,
      b_router_pad)


def _moe_body(be_ref, nv_ref, x_ref, wgu_ref, bgu_ref, wd_ref, bd_ref, o_ref, wgu_bf, wd_bf):
    i = pl.program_id(0)
    e = be_ref[i]
    e_prev = be_ref[jnp.maximum(i - 1, 0)]
    valid = i < nv_ref[0]

    @pl.when(valid & ((i == 0) | (e != e_prev)))
    def _():
        wgu_bf[...] = wgu_ref[0, 0].astype(BF16)
        wd_bf[...] = wd_ref[0, 0].astype(BF16)

    @pl.when(valid)
    def _():
        x = x_ref[...].astype(BF16)
        acc = jnp.zeros((MOE_BM, D_MODEL), F32) + bd_ref[0, 0]
        for c in range(D_EXPERT // F_CHUNK):
            lo = c * F_CHUNK
            gt = jnp.dot(x, wgu_bf[:, lo:lo + F_CHUNK], preferred_element_type=F32)
            gt = gt + bgu_ref[0, 0, :, lo:lo + F_CHUNK]
            up = jnp.dot(x, wgu_bf[:, D_EXPERT + lo:D_EXPERT + lo + F_CHUNK], preferred_element_type=F32)
            up = up + bgu_ref[0, 0, :, D_EXPERT + lo:D_EXPERT + lo + F_CHUNK]
            gt = jnp.minimum(gt, SWIGLU_LIMIT)
            up = jnp.clip(up, -SWIGLU_LIMIT, SWIGLU_LIMIT)
            act = (up + 1.0) * gt * jax.nn.sigmoid(SWIGLU_ALPHA * gt)
            acc = acc + jnp.dot(act.astype(BF16), wd_bf[lo:lo + F_CHUNK, :], preferred_element_type=F32)
        o_ref[...] = acc

    @pl.when(jnp.logical_not(valid))
    def _():
        o_ref[...] = jnp.zeros_like(o_ref)


def _moe_experts(layer, block_e, n_valid, xs, w_gate_up, b_gate_up, w_down, b_down):
    n_rows, d = xs.shape
    n_blocks = n_rows // MOE_BM
    f2 = 2 * D_EXPERT

    def blk(i, be, nv):
        return jnp.minimum(i, nv[0] - 1)

    grid_spec = pltpu.PrefetchScalarGridSpec(
        num_scalar_prefetch=2,
        grid=(n_blocks,),
        in_specs=[
            pl.BlockSpec((MOE_BM, d), lambda i, be, nv: (blk(i, be, nv), 0)),
            pl.BlockSpec((1, 1, d, f2), lambda i, be, nv: (layer, be[blk(i, be, nv)], 0, 0)),
            pl.BlockSpec((1, 1, 1, f2), lambda i, be, nv: (layer, be[blk(i, be, nv)], 0, 0)),
            pl.BlockSpec((1, 1, D_EXPERT, d), lambda i, be, nv: (layer, be[blk(i, be, nv)], 0, 0)),
            pl.BlockSpec((1, 1, 1, d), lambda i, be, nv: (layer, be[blk(i, be, nv)], 0, 0)),
        ],
        out_specs=pl.BlockSpec((MOE_BM, d), lambda i, be, nv: (i, 0)),
        scratch_shapes=[pltpu.VMEM((d, f2), BF16), pltpu.VMEM((D_EXPERT, d), BF16)],
    )
    return pl.pallas_call(
        _moe_body,
        out_shape=jax.ShapeDtypeStruct((n_rows, d), F32),
        grid_spec=grid_spec,
        compiler_params=pltpu.CompilerParams(dimension_semantics=("arbitrary",),
                                             vmem_limit_bytes=VMEM_LIMIT_BYTES),
        name="moe_experts",
    )(block_e, n_valid, xs, w_gate_up, b_gate_up.reshape(DEPTH, N_EXPERTS, 1, f2), w_down,
      b_down.reshape(DEPTH, N_EXPERTS, 1, d))


GATHER_ROWS = 512


def _gather_body(group, idx_ref, src_ref, o_ref, sem):
    d = src_ref.shape[1]

    def row_copy(src_row, r):
        dst = o_ref.at[pl.ds(r // group, 1), pl.ds((r % group) * d, d)]
        return pltpu.make_async_copy(src_ref.at[pl.ds(src_row, 1)], dst, sem.at[0])

    for r in range(GATHER_ROWS):
        row_copy(idx_ref[0, 0, r], r).start(priority=r % 2)
    for r in range(GATHER_ROWS):
        row_copy(0, r).wait()


def _gather_rows(src, idx, group=1):
    n_out = idx.shape[0]
    n_steps = n_out // GATHER_ROWS
    d = src.shape[1]
    return pl.pallas_call(
        functools.partial(_gather_body, group),
        out_shape=jax.ShapeDtypeStruct((n_out // group, group * d), src.dtype),
        grid=(n_steps,),
        in_specs=[
            pl.BlockSpec((1, 1, GATHER_ROWS), lambda i: (i, 0, 0), memory_space=pltpu.SMEM),
            pl.BlockSpec(memory_space=pl.ANY),
        ],
        out_specs=pl.BlockSpec((GATHER_ROWS // group, group * d), lambda i: (i, 0)),
        scratch_shapes=[pltpu.SemaphoreType.DMA((1,))],
        compiler_params=pltpu.CompilerParams(dimension_semantics=("arbitrary",)),
        name="gather_rows",
    )(idx.reshape(n_steps, 1, GATHER_ROWS), src)


def _combine_body(y_ref, g_ref, x_ref, mod_ref, ln_ref, o_ref):
    d = x_ref.shape[1]
    g = g_ref[...]
    f = g[:, 0:1] * y_ref[:, 0:d]
    for k in range(1, TOP_K):
        f = f + g[:, k:k + 1] * y_ref[:, k * d:(k + 1) * d]
    y = DN_ALPHA * x_ref[...] + mod_ref[0, 0:1, :] * f
    o_ref[...] = _ln_rows(y) * ln_ref[0:1, :] + ln_ref[1:2, :]


def _combine(y4, gate, x1, gate2, ln, rows_per_group):
    n, d = x1.shape
    n_groups = gate2.shape[0]
    row = lambda i: (i, 0)
    return pl.pallas_call(
        _combine_body,
        out_shape=jax.ShapeDtypeStruct((n, d), F32),
        grid=(n // ROW_TILE,),
        in_specs=[
            pl.BlockSpec((ROW_TILE, TOP_K * d), row),
            pl.BlockSpec((ROW_TILE, TOP_K), row),
            pl.BlockSpec((ROW_TILE, d), row),
            pl.BlockSpec((1, 1, d), lambda i: (_group_of_block(i, rows_per_group, n_groups), 0, 0)),
            pl.BlockSpec((2, d), lambda i: (0, 0)),
        ],
        out_specs=pl.BlockSpec((ROW_TILE, d), row),
        compiler_params=pltpu.CompilerParams(dimension_semantics=("arbitrary",),
                                             vmem_limit_bytes=VMEM_LIMIT_BYTES),
        name="combine",
    )(y4, gate, x1, gate2, ln)


def _moe(layer, h, logits, w_gate_up, b_gate_up, w_down, b_down):
    n_tok, d = h.shape
    top_logit, top_e = lax.top_k(logits, TOP_K)
    gate = jax.nn.softmax(top_logit, axis=-1)
    nk = n_tok * TOP_K
    flat_e = top_e.reshape(-1).astype(jnp.int32)
    onehot = (flat_e[:, None] == jnp.arange(N_EXPERTS, dtype=jnp.int32)[None, :]).astype(jnp.int32)
    csum = jnp.cumsum(onehot, axis=0)
    counts = csum[-1]
    rank = jnp.sum((csum - 1) * onehot, axis=1)
    padded = (counts + MOE_BM - 1) // MOE_BM * MOE_BM
    pad_end = jnp.cumsum(padded)
    pad_start = pad_end - padded
    dest = jnp.sum(onehot * pad_start[None, :], axis=1) + rank
    n_blocks = nk // MOE_BM + N_EXPERTS
    flat_tok = jnp.arange(nk, dtype=jnp.int32) // TOP_K
    buf_tok = jnp.zeros((n_blocks * MOE_BM,), jnp.int32).at[dest].set(flat_tok)
    block_start = jnp.arange(n_blocks, dtype=jnp.int32) * MOE_BM
    block_e = jnp.minimum(jnp.searchsorted(pad_end, block_start, side='right'), N_EXPERTS - 1).astype(jnp.int32)
    n_valid = (pad_end[-1] // MOE_BM).astype(jnp.int32).reshape(1)
    xs = _gather_rows(h, buf_tok)
    yb = _moe_experts(layer, block_e, n_valid, xs, w_gate_up, b_gate_up, w_down, b_down)
    return _gather_rows(yb, dest, group=TOP_K), gate


NA_ROWS_PER_STEP = 4
NA_PATTERN_ROWS = (0, 1, 2, 3, GRID_W // 2, GRID_W - 3, GRID_W - 2, GRID_W - 1)


def _na_bias_table(rpb):
    rows = GRID_W
    pat_r = np.array(NA_PATTERN_ROWS)
    r0 = np.clip(pat_r - NA_WIN_R // 2, 0, rows - NA_WIN_R)
    dr = r0[:, None] + np.arange(NA_WIN_R)[None, :] - pat_r[:, None]
    cidx = np.arange(GRID_W)
    c0 = np.clip(cidx - NA_WIN_C // 2, 0, GRID_W - NA_WIN_C)
    col_ok = (cidx[None, :] >= c0[:, None]) & (cidx[None, :] < c0[:, None] + NA_WIN_C)
    dc = np.clip(cidx[None, :] - cidx[:, None] + NA_WIN_C - 1, 0, 2 * NA_WIN_C - 2)
    n_dc = 2 * NA_WIN_C - 1
    sel = jnp.stack([jnp.stack([rpb[:, int(dr[p, i]) + NA_WIN_R - 1, :] for i in range(NA_WIN_R)], 1)
                     for p in range(len(NA_PATTERN_ROWS))], 0)
    onehot = jnp.asarray((dc[:, :, None] == np.arange(n_dc)[None, None, :]).astype(np.float32))
    bias = jnp.einsum('phic,qkc->phqik', sel, onehot, precision=_HI)
    bias = jnp.where(col_ok[None, None, :, None, :], bias, -jnp.inf)
    return bias.reshape(len(NA_PATTERN_ROWS), NA_HEADS * GRID_W, NA_WIN_R * GRID_W)


def _head_block_mask(n_q):
    shape = (NA_HEADS * n_q, NA_W)
    return (lax.broadcasted_iota(jnp.int32, shape, 0) // n_q) == (lax.broadcasted_iota(jnp.int32, shape, 1) // NA_HD)


def _heads_on_rows(q, mask):
    qt = jnp.concatenate([q] * NA_HEADS, axis=0)
    return jnp.where(mask, qt, jnp.zeros_like(qt)) * (NA_HD ** -0.5)


def _heads_to_lanes(o_all, n_q, n_heads, head_w):
    lane_h = lax.broadcasted_iota(jnp.int32, (n_q, n_heads * head_w), 1) // head_w
    out = jnp.zeros((n_q, n_heads * head_w), F32)
    for h in range(n_heads):
        out = out + jnp.where(lane_h == h, o_all[h * n_q:(h + 1) * n_q], 0.0)
    return out


def _na_body(q_ref, k_ref, v_ref, kc_ref, vc_ref, bias_ref, o_ref):
    g = pl.program_id(1)
    mask = _head_block_mask(GRID_W)
    kc = kc_ref[...]
    vc = vc_ref[...]
    rows = GRID_W
    for i in range(NA_ROWS_PER_STEP):
        r = g * NA_ROWS_PER_STEP + i
        r0 = jnp.clip(r - NA_WIN_R // 2, 0, rows - NA_WIN_R)
        pat = jnp.where(r < 4, r, jnp.where(r > rows - 4, r - (rows - 8), 4))
        qb = _heads_on_rows(q_ref[i * GRID_W:(i + 1) * GRID_W, :], mask)
        start = pl.multiple_of(r0 * GRID_W, GRID_W)
        ks = k_ref[pl.ds(start, NA_WIN_R * GRID_W), :]
        vs = v_ref[pl.ds(start, NA_WIN_R * GRID_W), :]
        s_loc = lax.dot_general(qb, ks, _NT, preferred_element_type=F32) + bias_ref[pat]
        s_ctx = lax.dot_general(qb, kc, _NT, preferred_element_type=F32)
        m = jnp.maximum(jnp.max(s_loc, axis=-1, keepdims=True), jnp.max(s_ctx, axis=-1, keepdims=True))
        p_loc = jnp.exp(s_loc - m)
        p_ctx = jnp.exp(s_ctx - m)
        denom = jnp.sum(p_loc, axis=-1, keepdims=True) + jnp.sum(p_ctx, axis=-1, keepdims=True)
        o_all = (jnp.dot(p_loc.astype(BF16), vs, preferred_element_type=F32)
                 + jnp.dot(p_ctx.astype(BF16), vc, preferred_element_type=F32)) / denom
        o_ref[i * GRID_W:(i + 1) * GRID_W, :] = _heads_to_lanes(o_all, GRID_W, NA_HEADS, NA_HD).astype(o_ref.dtype)


def _na_latent(qkv, bias, b_, S, L):
    tq = NA_ROWS_PER_STEP * GRID_W
    steps = S // tq
    ctx_blk0 = b_ * S // L
    return pl.pallas_call(
        _na_body,
        out_shape=jax.ShapeDtypeStruct((b_ * S, NA_W), BF16),
        grid=(b_, steps),
        in_specs=[
            pl.BlockSpec((tq, NA_W), lambda b, g: (b * steps + g, 0)),
            pl.BlockSpec((S, NA_W), lambda b, g: (b, 1)),
            pl.BlockSpec((S, NA_W), lambda b, g: (b, 2)),
            pl.BlockSpec((L, NA_W), lambda b, g: (ctx_blk0 + b, 1)),
            pl.BlockSpec((L, NA_W), lambda b, g: (ctx_blk0 + b, 2)),
            pl.BlockSpec(bias.shape, lambda b, g: (0, 0, 0)),
        ],
        out_specs=pl.BlockSpec((tq, NA_W), lambda b, g: (b * steps + g, 0)),
        compiler_params=pltpu.CompilerParams(dimension_semantics=("arbitrary", "arbitrary"),
                                             vmem_limit_bytes=VMEM_LIMIT_BYTES),
        name="na_latent",
    )(qkv, qkv, qkv, qkv, qkv, bias)


def _na_ctx_body(q_ref, k_ref, v_ref, o_ref):
    n_q = q_ref.shape[0]
    qb = _heads_on_rows(q_ref[...], _head_block_mask(n_q))
    s = lax.dot_general(qb, k_ref[...], _NT, preferred_element_type=F32)
    p = jnp.exp(s - jnp.max(s, axis=-1, keepdims=True))
    o_all = jnp.dot(p.astype(BF16), v_ref[...], preferred_element_type=F32) / jnp.sum(p, axis=-1, keepdims=True)
    o_ref[...] = _heads_to_lanes(o_all, n_q, NA_HEADS, NA_HD).astype(o_ref.dtype)


def _na_context(qkv, b_, S, L):
    ctx_blk0 = b_ * S // L
    return pl.pallas_call(
        _na_ctx_body,
        out_shape=jax.ShapeDtypeStruct((b_ * L, NA_W), BF16),
        grid=(b_,),
        in_specs=[pl.BlockSpec((L, NA_W), lambda b: (ctx_blk0 + b, 0)),
                  pl.BlockSpec((L, NA_W), lambda b: (ctx_blk0 + b, 1)),
                  pl.BlockSpec((L, NA_W), lambda b: (ctx_blk0 + b, 2))],
        out_specs=pl.BlockSpec((L, NA_W), lambda b: (b, 0)),
        compiler_params=pltpu.CompilerParams(dimension_semantics=("arbitrary",)),
        name="na_context",
    )(qkv, qkv, qkv)


GLA_BLOCK = 256
GLA_SUB = 8


def _rope_lane_tables(S, L):
    half = GLA_DK // 2
    inv_freq = ROPE_BASE ** (-jnp.arange(0, half, 2, dtype=F32) / half)
    t = jnp.arange(S)
    d = np.arange(GLA_QK_W) % GLA_DK
    use_col = d >= half
    fidx = d % (half // 2)
    sign = np.where((d % half) < half // 2, -1.0, 1.0).astype(np.float32)
    pos = jnp.where(use_col[None, :], (t % GRID_W)[:, None], (t // GRID_W)[:, None]).astype(F32)
    ang = pos * inv_freq[fidx][None, :]
    cos = jnp.concatenate([jnp.cos(ang), jnp.ones((L, GLA_QK_W), F32)], 0)
    sin = jnp.concatenate([jnp.sin(ang) * sign[None, :], jnp.zeros((L, GLA_QK_W), F32)], 0)
    return cos, sin


def _rope_lanes(x, cos, sin):
    pair = GLA_DK // 4
    lane = lax.broadcasted_iota(jnp.int32, x.shape, 1)
    first = (lane % (2 * pair)) < pair
    partner = jnp.where(first, pltpu.roll(x, LANES - pair, 1), pltpu.roll(x, pair, 1))
    return x * cos + partner * sin


def _bcast_rows(x, r, n):
    return jnp.broadcast_to(x[r:r + 1, :], (n, x.shape[1]))


def _gla_consts(rev):
    C = GLA_CHUNK
    r = lax.broadcasted_iota(jnp.int32, (C, C), 0)
    c = lax.broadcasted_iota(jnp.int32, (C, C), 1)
    tri = jnp.where((c >= r) if rev else (c <= r), 1.0, 0.0).astype(F32)
    n_exp = (C // GLA_SUB - 1) * GLA_QK_W
    head_mask_q = ((lax.broadcasted_iota(jnp.int32, (GLA_HEADS * C, n_exp), 0) // C)
                   == ((lax.broadcasted_iota(jnp.int32, (GLA_HEADS * C, n_exp), 1) % GLA_QK_W) // GLA_DK))
    s3 = ((lax.broadcasted_iota(jnp.int32, (GLA_QK_W, GLA_V_W), 0) // GLA_DK)
          == (lax.broadcasted_iota(jnp.int32, (GLA_QK_W, GLA_V_W), 1) // GLA_DV)).astype(BF16)
    n_p = C * GLA_SUB
    rsum = (lax.broadcasted_iota(jnp.int32, (C, n_p), 0)
            == lax.broadcasted_iota(jnp.int32, (C, n_p), 1) // GLA_SUB).astype(BF16)
    row = lax.broadcasted_iota(jnp.int32, (n_p, GLA_QK_W), 0)
    t_loc, s_loc = (row // GLA_SUB) % GLA_SUB, row % GLA_SUB
    diag_ok = (s_loc >= t_loc) if rev else (s_loc <= t_loc)
    st_mask = ((lax.broadcasted_iota(jnp.int32, (GLA_V_W, GLA_QK_W), 0) // GLA_DV)
               == (lax.broadcasted_iota(jnp.int32, (GLA_V_W, GLA_QK_W), 1) // GLA_DK))
    return tri, head_mask_q, s3, rsum, diag_ok, st_mask


def _gla_chunk(q, k, v, la, st, rev, consts):
    tri, head_mask_q, s3, rsum, diag_ok, st_mask = consts
    C, nb = GLA_CHUNK, GLA_CHUNK // GLA_SUB
    gc = jnp.dot(tri, la, precision=_HI, preferred_element_type=F32)
    zeros = jnp.zeros((GLA_SUB, GLA_QK_W), F32)
    if not rev:
        g_tot = gc[C - 1:C]
        ref_rows = [None] + [GLA_SUB * i - 1 for i in range(1, nb)]
        blk_rows = [GLA_SUB * j + GLA_SUB - 1 for j in range(nb)]
    else:
        g_tot = gc[0:1]
        ref_rows = [GLA_SUB * (i + 1) for i in range(nb - 1)] + [None]
        blk_rows = [GLA_SUB * j for j in range(nb)]
    g_ref = jnp.concatenate([zeros if r is None else _bcast_rows(gc, r, GLA_SUB) for r in ref_rows], 0)
    g_blk = jnp.concatenate([_bcast_rows(gc, r, GLA_SUB) for r in blk_rows], 0)
    qt = q * jnp.exp(gc - g_ref)
    kh = k * jnp.exp(g_blk - gc)
    sub = lax.broadcasted_iota(jnp.int32, (C, GLA_QK_W), 0) // GLA_SUB
    k_slabs, q_slabs = [], []
    for i in range(nb):
        if ref_rows[i] is None:
            continue
        keys_ok = (sub > i) if rev else (sub < i)
        between = jnp.minimum(gc[ref_rows[i]:ref_rows[i] + 1] - g_blk, 0.0)
        k_slabs.append(jnp.where(keys_ok, kh * jnp.exp(between), 0.0))
        q_slabs.append(jnp.where(sub == i, qt, 0.0))
    k_exp = jnp.concatenate(k_slabs, 1).astype(BF16)
    q_exp = jnp.concatenate(q_slabs, 1)
    q_exp = jnp.where(head_mask_q, jnp.concatenate([q_exp] * GLA_HEADS, 0), 0.0).astype(BF16)
    attn_off = lax.dot_general(q_exp, k_exp, _NT, preferred_element_type=F32)
    vb = v.astype(BF16)
    o_off_all = jnp.dot(attn_off.astype(BF16), vb, preferred_element_type=F32)
    o = _heads_to_lanes(o_off_all, C, GLA_HEADS, GLA_DV)
    pieces = []
    for i in range(nb):
        lo = i * GLA_SUB
        k_i, g_i = k[lo:lo + GLA_SUB], gc[lo:lo + GLA_SUB]
        for t in range(GLA_SUB):
            e = jnp.exp(jnp.minimum(_bcast_rows(gc, lo + t, GLA_SUB) - g_i, 0.0))
            pieces.append(_bcast_rows(q, lo + t, GLA_SUB) * k_i * e)
    p = jnp.where(diag_ok, jnp.concatenate(pieces, 0), 0.0).astype(BF16)
    a2 = jnp.dot(p, s3, preferred_element_type=F32)
    v_rep = jnp.concatenate([v[i * GLA_SUB:(i + 1) * GLA_SUB] for i in range(nb) for _ in range(GLA_SUB)], 0)
    o = o + jnp.dot(rsum, (a2 * v_rep).astype(BF16), preferred_element_type=F32)
    q_dec = (q * jnp.exp(gc)).astype(BF16)
    k_dec = (k * jnp.exp(g_tot - gc)).astype(BF16)
    o = o + lax.dot_general(q_dec, st.astype(BF16), _NT, preferred_element_type=F32)
    upd = lax.dot_general(vb, k_dec, _TN, preferred_element_type=F32)
    return o, st * jnp.exp(g_tot) + jnp.where(st_mask, upd, 0.0)


def _gla_body(xf_ref, xb_ref, cf_ref, sf_ref, cb_ref, sb_ref, w2_ref, b2_ref, of_ref, ob_ref,
              q_s, k_s, la_s, st_s):
    @pl.when(pl.program_id(1) == 0)
    def _():
        st_s[...] = jnp.zeros_like(st_s)

    gate_lo = 2 * GLA_QK_W + 2 * GLA_V_W
    for d, (x_ref, cos_ref, sin_ref) in enumerate(((xf_ref, cf_ref, sf_ref), (xb_ref, cb_ref, sb_ref))):
        z = jnp.dot(x_ref[:, gate_lo:], w2_ref[...], precision=_HI, preferred_element_type=F32) + b2_ref[...]
        la_s[d] = jax.nn.log_sigmoid(z[:, d * GLA_QK_W:(d + 1) * GLA_QK_W]) * (1.0 / GLA_TAU)
        q_s[d] = _rope_lanes(x_ref[:, 0:GLA_QK_W], cos_ref[...], sin_ref[...]) * (GLA_DK ** -0.5)
        k_s[d] = _rope_lanes(x_ref[:, GLA_QK_W:2 * GLA_QK_W], cos_ref[...], sin_ref[...])

    consts = (_gla_consts(False), _gla_consts(True))
    n_chunks = GLA_BLOCK // GLA_CHUNK

    def step(i, carry):
        starts = (i * GLA_CHUNK, (n_chunks - 1 - i) * GLA_CHUNK)
        for d, (x_ref, o_ref) in enumerate(((xf_ref, of_ref), (xb_ref, ob_ref))):
            rows = pl.ds(pl.multiple_of(starts[d], GLA_CHUNK), GLA_CHUNK)
            v = x_ref[rows, 2 * GLA_QK_W:2 * GLA_QK_W + GLA_V_W]
            o, st = _gla_chunk(q_s[d, rows, :], k_s[d, rows, :], v, la_s[d, rows, :], st_s[d], d == 1, consts[d])
            o_ref[rows, :] = o
            st_s[d] = st
        return carry

    lax.fori_loop(0, n_chunks, step, 0, unroll=2)


def _gla_gate_weights(w_a2, b_a):
    w2 = jnp.zeros((LANES, 2 * GLA_QK_W), F32)
    w2 = w2.at[0:GLA_RANK, 0:GLA_QK_W].set(w_a2[0]).at[GLA_RANK:2 * GLA_RANK, GLA_QK_W:].set(w_a2[1])
    return w2, b_a.reshape(1, 2 * GLA_QK_W)


def _gla_scan(cols, cos, sin, w2, b2, b_, S, L):
    n = cols.shape[0]
    lat_blocks = S // GLA_BLOCK
    ctx0 = b_ * S // GLA_BLOCK
    assert L == GLA_BLOCK
    fwd = lambda b, j: jnp.where(j == 0, ctx0 + b, b * lat_blocks + j - 1)
    bwd = lambda b, j: jnp.where(j == 0, ctx0 + b, b * lat_blocks + lat_blocks - j)
    fwd_t = lambda b, j: jnp.where(j == 0, lat_blocks, j - 1)
    bwd_t = lambda b, j: jnp.where(j == 0, lat_blocks, lat_blocks - j)
    blk = lambda f: (lambda b, j: (f(b, j), 0))
    const = lambda b, j: (0, 0)
    return pl.pallas_call(
        _gla_body,
        out_shape=(jax.ShapeDtypeStruct((n, GLA_V_W), F32), jax.ShapeDtypeStruct((n, GLA_V_W), F32)),
        grid=(b_, lat_blocks + 1),
        in_specs=[
            pl.BlockSpec((GLA_BLOCK, GLA_SEG_W), blk(fwd)),
            pl.BlockSpec((GLA_BLOCK, GLA_SEG_W), blk(bwd)),
            pl.BlockSpec((GLA_BLOCK, GLA_QK_W), blk(fwd_t)),
            pl.BlockSpec((GLA_BLOCK, GLA_QK_W), blk(fwd_t)),
            pl.BlockSpec((GLA_BLOCK, GLA_QK_W), blk(bwd_t)),
            pl.BlockSpec((GLA_BLOCK, GLA_QK_W), blk(bwd_t)),
            pl.BlockSpec((LANES, 2 * GLA_QK_W), const),
            pl.BlockSpec((1, 2 * GLA_QK_W), const),
        ],
        out_specs=(pl.BlockSpec((GLA_BLOCK, GLA_V_W), blk(fwd)), pl.BlockSpec((GLA_BLOCK, GLA_V_W), blk(bwd))),
        scratch_shapes=[pltpu.VMEM((2, GLA_BLOCK, GLA_QK_W), F32), pltpu.VMEM((2, GLA_BLOCK, GLA_QK_W), F32),
                        pltpu.VMEM((2, GLA_BLOCK, GLA_QK_W), F32), pltpu.VMEM((2, GLA_V_W, GLA_QK_W), F32)],
        compiler_params=pltpu.CompilerParams(dimension_semantics=("arbitrary", "arbitrary"),
                                             vmem_limit_bytes=VMEM_LIMIT_BYTES),
        name="gla_scan",
    )(cols, cols, cos, sin, cos, sin, w2, b2)


SUBLANES = 8
GDN_CONV_W = 2 * GDN_QK_W + GDN_V_W
GDN_GATE_LO = 2 * GDN_QK_W + 2 * GDN_V_W
GDN_BLOCK = 256
GDN_HT = GDN_HEADS * GDN_CHUNK


def _stack_heads(x):
    return jnp.concatenate([x[:, h * GDN_DK:(h + 1) * GDN_DK] for h in range(GDN_HEADS)], 0)


def _stack_cols(a, lane0, width):
    return jnp.concatenate([jnp.broadcast_to(a[:, lane0 + h:lane0 + h + 1], (a.shape[0], width))
                            for h in range(GDN_HEADS)], 0)


def _block_diag(xs, mask):
    return jnp.where(mask, jnp.concatenate([xs] * GDN_HEADS, 1), 0.0).astype(BF16)


def _gdn_consts(rev):
    C, HT = GDN_CHUNK, GDN_HT
    r = lax.broadcasted_iota(jnp.int32, (C, C), 0)
    c = lax.broadcasted_iota(jnp.int32, (C, C), 1)
    tri = jnp.where((c >= r) if rev else (c <= r), 1.0, 0.0).astype(F32)
    rr = lax.broadcasted_iota(jnp.int32, (HT, HT), 0)
    cc = lax.broadcasted_iota(jnp.int32, (HT, HT), 1)
    same_head = (rr // C) == (cc // C)
    t, s = rr % C, cc % C
    strict = same_head & ((s > t) if rev else (s < t))
    incl = same_head & ((s >= t) if rev else (s <= t))
    bd_mask = ((lax.broadcasted_iota(jnp.int32, (HT, GDN_QK_W), 0) // C)
               == (lax.broadcasted_iota(jnp.int32, (HT, GDN_QK_W), 1) // GDN_DK))
    pick_row0 = (lax.broadcasted_iota(jnp.int32, (SUBLANES, HT), 1) == 0).astype(F32)
    return tri, strict, incl, bd_mask, pick_row0


def _gdn_chunk(qs, ks, vs, gcol, bcol, st, rev, consts):
    tri, strict, incl, bd_mask, pick_row0 = consts
    C = GDN_CHUNK
    g_row = lax.dot_general(pick_row0, gcol, _NT, precision=_HI, preferred_element_type=F32)[0:1]
    dec = jnp.exp(jnp.minimum(gcol - g_row, 0.0))
    kb = ks * bcol
    k_bd = _block_diag(ks, bd_mask)
    n = jnp.where(strict, lax.dot_general(_block_diag(kb, bd_mask), k_bd, _NT, preferred_element_type=F32) * dec, 0.0)
    qk = jnp.where(incl, lax.dot_general(_block_diag(qs, bd_mask), k_bd, _NT, preferred_element_type=F32) * dec, 0.0)
    tp = -n
    p = jnp.dot(n.astype(BF16), n.astype(BF16), preferred_element_type=F32)
    n_factors = int(np.log2(C)) - 1
    for it in range(n_factors):
        pb = p.astype(BF16)
        tp = tp + p + jnp.dot(tp.astype(BF16), pb, preferred_element_type=F32)
        if it < n_factors - 1:
            p = jnp.dot(pb, pb, preferred_element_type=F32)
    g = gcol[:, :GDN_DK]
    last = 0 if rev else C - 1
    g_last = jnp.concatenate([_bcast_rows(g, h * C + last, C) for h in range(GDN_HEADS)], 0)
    eg = jnp.exp(g)
    q_dec = qs * eg
    kbe = kb * eg
    k_dec = ks * jnp.exp(g_last - g)
    stb = st.astype(BF16)
    r = vs * bcol - jnp.dot(_block_diag(kbe, bd_mask), stb, preferred_element_type=F32)
    v_new = r + jnp.dot(tp.astype(BF16), r.astype(BF16), preferred_element_type=F32)
    vnb = v_new.astype(BF16)
    o = (jnp.dot(_block_diag(q_dec, bd_mask), stb, preferred_element_type=F32)
         + jnp.dot(qk.astype(BF16), vnb, preferred_element_type=F32))
    a_last = jnp.exp(jnp.concatenate([g_last[h * C:(h + 1) * C] for h in range(GDN_HEADS)
                                      for _ in range(GDN_DK // C)], 0))
    st_new = st * a_last + lax.dot_general(_block_diag(k_dec, bd_mask), vnb, _TN, preferred_element_type=F32)
    return o, st_new


def _gdn_prep(x_ref, prev_ref, next_ref, has_prev, has_next, w_ref, nega_ref, dt_ref):
    pad = GDN_CONV // 2
    cur = x_ref[:, 0:GDN_CONV_W]
    before = jnp.where(has_prev, prev_ref[SUBLANES - pad:SUBLANES, 0:GDN_CONV_W], 0.0)
    after = jnp.where(has_next, next_ref[0:pad, 0:GDN_CONV_W], 0.0)
    xe = jnp.concatenate([before, cur, after], 0)
    y = w_ref[0:1, :] * xe[0:GDN_BLOCK]
    for j in range(1, GDN_CONV):
        y = y + w_ref[j:j + 1, :] * xe[j:j + GDN_BLOCK]
    y = y * jax.nn.sigmoid(y)

    def l2n(t):
        parts = []
        for h in range(GDN_HEADS):
            seg = t[:, h * GDN_DK:(h + 1) * GDN_DK]
            parts.append(seg * lax.rsqrt(jnp.sum(seg * seg, -1, keepdims=True) + 1e-6))
        return jnp.concatenate(parts, 1)

    q = l2n(y[:, 0:GDN_QK_W]) * (GDN_DK ** -0.5)
    k = l2n(y[:, GDN_QK_W:2 * GDN_QK_W])
    v = y[:, 2 * GDN_QK_W:]
    tail = x_ref[:, GDN_GATE_LO:]
    beta = jax.nn.sigmoid(tail)
    tb = tail + dt_ref[...]
    softplus = jnp.maximum(tb, 0.0) + jnp.log(1.0 + jnp.exp(-jnp.abs(tb)))
    return q, k, v, beta, nega_ref[...] * softplus


def _gdn_body(lat_blocks, xf_ref, pf_ref, nf_ref, xb_ref, pb_ref, nb_ref, w_ref, nega_ref, dt_ref, of_ref, ob_ref,
              q_s, k_s, v_s, b_s, g_s, st_s):
    j = pl.program_id(1)

    @pl.when(j == 0)
    def _():
        st_s[...] = jnp.zeros_like(st_s)

    t_blk = (j - 1, lat_blocks - j)
    for d, (x_ref, p_ref, n_ref) in enumerate(((xf_ref, pf_ref, nf_ref), (xb_ref, pb_ref, nb_ref))):
        has_prev = (j > 0) & (t_blk[d] > 0)
        has_next = (j > 0) & (t_blk[d] < lat_blocks - 1)
        q_s[d], k_s[d], v_s[d], b_s[d], g_s[d] = _gdn_prep(x_ref, p_ref, n_ref, has_prev, has_next,
                                                           w_ref, nega_ref, dt_ref)

    consts = (_gdn_consts(False), _gdn_consts(True))
    n_chunks = GDN_BLOCK // GDN_CHUNK

    def step(i, carry):
        starts = (i * GDN_CHUNK, (n_chunks - 1 - i) * GDN_CHUNK)
        for d, o_ref in enumerate((of_ref, ob_ref)):
            rows = pl.ds(pl.multiple_of(starts[d], GDN_CHUNK), GDN_CHUNK)
            gc = jnp.dot(consts[d][0], g_s[d, rows, :], precision=_HI, preferred_element_type=F32)
            gcol = _stack_cols(gc, (2 + d) * GDN_HEADS, GDN_HT)
            bcol = _stack_cols(b_s[d, rows, :], d * GDN_HEADS, GDN_DK)
            o, st = _gdn_chunk(_stack_heads(q_s[d, rows, :]), _stack_heads(k_s[d, rows, :]),
                               _stack_heads(v_s[d, rows, :]), gcol, bcol, st_s[d], d == 1, consts[d])
            o_ref[rows, :] = jnp.concatenate([o[h * GDN_CHUNK:(h + 1) * GDN_CHUNK] for h in range(GDN_HEADS)], 1)
            st_s[d] = st
        return carry

    lax.fori_loop(0, n_chunks, step, 0)


def _gdn_gate_consts(a_log, dt_bias):
    nega = jnp.zeros((1, LANES), F32).at[0, 2 * GDN_HEADS:4 * GDN_HEADS].set(-jnp.exp(a_log).reshape(-1))
    dt = jnp.zeros((1, LANES), F32).at[0, 2 * GDN_HEADS:4 * GDN_HEADS].set(dt_bias.reshape(-1))
    return nega, dt


def _gdn_scan(cols, conv_w, nega, dt, b_, S, L):
    n = cols.shape[0]
    lat_blocks = S // GDN_BLOCK
    ctx0 = b_ * S // GDN_BLOCK
    assert L == GDN_BLOCK
    per8 = GDN_BLOCK // SUBLANES
    n8 = n // SUBLANES
    fwd = lambda b, j: jnp.where(j == 0, ctx0 + b, b * lat_blocks + j - 1)
    bwd = lambda b, j: jnp.where(j == 0, ctx0 + b, b * lat_blocks + lat_blocks - j)
    blk = lambda f: (lambda b, j: (f(b, j), 0))
    prev8 = lambda f: (lambda b, j: (jnp.maximum(f(b, j) * per8 - 1, 0), 0))
    next8 = lambda f: (lambda b, j: (jnp.minimum(f(b, j) * per8 + per8, n8 - 1), 0))
    const = lambda b, j: (0, 0)
    x_spec = lambda f: pl.BlockSpec((GDN_BLOCK, GDN_SEG_W), blk(f))
    halo = lambda m: pl.BlockSpec((SUBLANES, GDN_SEG_W), m)
    return pl.pallas_call(
        functools.partial(_gdn_body, lat_blocks),
        out_shape=(jax.ShapeDtypeStruct((n, GDN_V_W), F32), jax.ShapeDtypeStruct((n, GDN_V_W), F32)),
        grid=(b_, lat_blocks + 1),
        in_specs=[x_spec(fwd), halo(prev8(fwd)), halo(next8(fwd)), x_spec(bwd), halo(prev8(bwd)), halo(next8(bwd)),
                  pl.BlockSpec((GDN_CONV, GDN_CONV_W), const), pl.BlockSpec((1, LANES), const),
                  pl.BlockSpec((1, LANES), const)],
        out_specs=(pl.BlockSpec((GDN_BLOCK, GDN_V_W), blk(fwd)), pl.BlockSpec((GDN_BLOCK, GDN_V_W), blk(bwd))),
        scratch_shapes=[pltpu.VMEM((2, GDN_BLOCK, GDN_QK_W), F32), pltpu.VMEM((2, GDN_BLOCK, GDN_QK_W), F32),
                        pltpu.VMEM((2, GDN_BLOCK, GDN_V_W), F32), pltpu.VMEM((2, GDN_BLOCK, LANES), F32),
                        pltpu.VMEM((2, GDN_BLOCK, LANES), F32), pltpu.VMEM((2, GDN_QK_W, GDN_DV), F32)],
        compiler_params=pltpu.CompilerParams(dimension_semantics=("arbitrary", "arbitrary"),
                                             vmem_limit_bytes=VMEM_LIMIT_BYTES),
        name="gdn_scan",
    )(cols, cols, cols, cols, cols, cols, conv_w, nega, dt)


def kernel(x, c, ctx, c_ctx, w_ada, b_ada, w_in, gla_w_a2, gla_b_a, gla_norm, na_rpb, gdn_conv, gdn_a_log,
           gdn_dt_bias, gdn_norm, w_out, ln1_g, ln1_b, w_router, b_router, w_gate_up, b_gate_up, w_down, b_down,
           ln2_g, ln2_b):
    b_, S, D = x.shape
    L = ctx.shape[1]
    n_lat, n_ctx = b_ * S, b_ * L
    rope_cos, rope_sin = _rope_lane_tables(S, L)
    sc = jnp.concatenate([jax.nn.silu(c), jax.nn.silu(c_ctx)[None, :]], 0)
    xt = jnp.concatenate([x.reshape(n_lat, D), ctx.reshape(n_ctx, D)], 0)
    for l in range(DEPTH):
        last = l == DEPTH - 1
        m = (jnp.dot(sc, w_ada[l], precision=_HI) + b_ada[l]).reshape(b_ + 1, N_MOD, D)
        shift1, scale1, gate1, shift2, scale2, gate2 = (m[:, j] for j in range(N_MOD))
        gla_cols, na_cols, gdn_cols = _inproj(xt, jnp.stack([shift1, 1.0 + scale1], 1), _inproj_weights(w_in[l]), S)
        gla_w2, gla_b2 = _gla_gate_weights(gla_w_a2[l], gla_b_a[l])
        gla_f, gla_b = _gla_scan(gla_cols, rope_cos, rope_sin, gla_w2, gla_b2, b_, S, L)
        na = jnp.concatenate([_na_latent(na_cols, _na_bias_table(na_rpb[l]), b_, S, L),
                              _na_context(na_cols, b_, S, L)], 0)
        gdn_nega, gdn_dt = _gdn_gate_consts(gdn_a_log[l], gdn_dt_bias[l])
        gdn_f, gdn_b = _gdn_scan(gdn_cols, gdn_conv[l], gdn_nega, gdn_dt, b_, S, L)
        n_rows = n_lat if last else n_lat + n_ctx
        w_router_pad = jnp.pad(w_router[l], ((0, 0), (0, LANES - N_EXPERTS)))
        b_router_pad = jnp.pad(b_router[l], (0, LANES - N_EXPERTS)).reshape(1, LANES)
        x1, h2, logits = _outproj(n_rows, gla_f, gla_b, gla_cols, jnp.tile(gla_norm[l], GLA_HEADS)[None, :], na,
                                  gdn_f, gdn_b, gdn_cols, jnp.tile(gdn_norm[l], GDN_HEADS)[None, :], xt,
                                  jnp.stack([gate1, shift2, 1.0 + scale2], 1),
                                  jnp.stack([ln1_g[l], ln1_b[l]], 0), w_out[l].astype(BF16),
                                  w_router_pad, b_router_pad, S)
        y4, gate = _moe(l, h2, logits[:, :N_EXPERTS], w_gate_up, b_gate_up, w_down, b_down)
        xt = _combine(y4, gate, x1, gate2[:, None, :], jnp.stack([ln2_g[l], ln2_b[l]], 0), S)
    return xt[:n_lat].reshape(b_, S, D)
```

```python
import functools

import numpy as np

import jax
import jax.numpy as jnp
from jax import lax
from jax.experimental import pallas as pl
from jax.experimental.pallas import tpu as pltpu

D_MODEL = 1024
DEPTH = 4
GRID_W = 64
GLA_HEADS, GLA_DK, GLA_DV, GLA_RANK, GLA_TAU, GLA_CHUNK = 4, 32, 64, 16, 16.0, 32
ROPE_BASE = 10000.0
NA_HEADS, NA_HD, NA_WIN_R, NA_WIN_C = 4, 64, 8, 16
GDN_HEADS, GDN_DK, GDN_DV, GDN_CONV, GDN_CHUNK = 4, 128, 128, 5, 64
N_EXPERTS, TOP_K, D_EXPERT = 32, 4, 1024
SWIGLU_LIMIT, SWIGLU_ALPHA = 7.0, 1.702
N_MOD = 6
LN_EPS, RMS_EPS = 1e-5, 1e-6
DN_ALPHA = (2 * DEPTH) ** 0.25

F32 = jnp.float32
BF16 = jnp.bfloat16

LANES = 128
VMEM_LIMIT_BYTES = 56 * 1024 * 1024
ROW_TILE = 512
MOE_BM = 512
F_CHUNK = 512

GLA_QK_W = GLA_HEADS * GLA_DK
GLA_V_W = GLA_HEADS * GLA_DV
NA_W = NA_HEADS * NA_HD
GDN_QK_W = GDN_HEADS * GDN_DK
GDN_V_W = GDN_HEADS * GDN_DV
GLA_SEG_W = 2 * GLA_QK_W + 2 * GLA_V_W + LANES
NA_SEG_W = 3 * NA_W
GDN_SEG_W = 2 * GDN_QK_W + 2 * GDN_V_W + LANES
D_IN_SEG = GLA_SEG_W + NA_SEG_W + GDN_SEG_W
MIX_W = GLA_V_W + NA_W + GDN_V_W

_NT = (((1,), (1,)), ((), ()))
_TN = (((0,), (0,)), ((), ()))
_HI = lax.Precision.HIGHEST


def _ln_rows(x):
    mu = jnp.mean(x, axis=-1, keepdims=True)
    xc = x - mu
    var = jnp.mean(xc * xc, axis=-1, keepdims=True)
    return xc * lax.rsqrt(var + LN_EPS)


def _group_of_block(i, rows_per_group, n_groups):
    return jnp.minimum(i * ROW_TILE // rows_per_group, n_groups - 1)


def _inproj_weights(w):
    o = np.cumsum((0, GLA_QK_W, GLA_QK_W, GLA_V_W, 2 * GLA_RANK, GLA_V_W, NA_W, NA_W, NA_W,
                   GDN_QK_W, GDN_QK_W, GDN_V_W, 2 * GDN_HEADS, 2 * GDN_HEADS, GDN_V_W)).tolist()
    col = lambda i: w[:, o[i]:o[i + 1]]
    zeros = lambda n: jnp.zeros((w.shape[0], n), w.dtype)
    parts = [col(0), col(1), col(2), col(4), col(3), zeros(LANES - 2 * GLA_RANK),
             col(5), col(6), col(7),
             col(8), col(9), col(10), col(13), col(11), col(12), zeros(LANES - 4 * GDN_HEADS)]
    return jnp.concatenate(parts, axis=1).astype(BF16)


def _inproj_body(x_ref, mod_ref, w_ref, gla_ref, na_ref, gdn_ref):
    h = _ln_rows(x_ref[...]) * mod_ref[0, 1:2, :] + mod_ref[0, 0:1, :]
    hb = h.astype(BF16)
    gla_ref[...] = jnp.dot(hb, w_ref[:, 0:GLA_SEG_W], preferred_element_type=F32)
    na_ref[...] = jnp.dot(hb, w_ref[:, GLA_SEG_W:GLA_SEG_W + NA_SEG_W], preferred_element_type=F32).astype(BF16)
    gdn_ref[...] = jnp.dot(hb, w_ref[:, GLA_SEG_W + NA_SEG_W:], preferred_element_type=F32)


def _inproj(x, mod, w_bf, rows_per_group):
    n, d = x.shape
    n_groups = mod.shape[0]
    row = lambda i: (i, 0)
    return pl.pallas_call(
        _inproj_body,
        out_shape=(jax.ShapeDtypeStruct((n, GLA_SEG_W), F32), jax.ShapeDtypeStruct((n, NA_SEG_W), BF16),
                   jax.ShapeDtypeStruct((n, GDN_SEG_W), F32)),
        grid=(n // ROW_TILE,),
        in_specs=[
            pl.BlockSpec((ROW_TILE, d), row),
            pl.BlockSpec((1, 2, d), lambda i: (_group_of_block(i, rows_per_group, n_groups), 0, 0)),
            pl.BlockSpec((d, D_IN_SEG), lambda i: (0, 0)),
        ],
        out_specs=(pl.BlockSpec((ROW_TILE, GLA_SEG_W), row), pl.BlockSpec((ROW_TILE, NA_SEG_W), row),
                   pl.BlockSpec((ROW_TILE, GDN_SEG_W), row)),
        compiler_params=pltpu.CompilerParams(dimension_semantics=("arbitrary",),
                                             vmem_limit_bytes=VMEM_LIMIT_BYTES),
        name="inproj",
    )(x, mod, w_bf)


def _outproj_body(gf_ref, gb_ref, gg_ref, gn_ref, na_ref, df_ref, db_ref, dz_ref, dn_ref, x_ref, mod_ref, ln_ref,
                  w_ref, wr_ref, br_ref, xo_ref, h_ref, lg_ref):
    o = gf_ref[...] + gb_ref[...]
    sq = o * o
    sq_hi = sq.astype(BF16)
    sq_lo = (sq - sq_hi.astype(F32)).astype(BF16)
    head_mean = ((lax.broadcasted_iota(jnp.int32, (GLA_V_W, GLA_V_W), 0) // GLA_DV)
                 == (lax.broadcasted_iota(jnp.int32, (GLA_V_W, GLA_V_W), 1) // GLA_DV)
                 ).astype(BF16) * (1.0 / GLA_DV)
    ms = (jnp.dot(sq_hi, head_mean, preferred_element_type=F32)
          + jnp.dot(sq_lo, head_mean, preferred_element_type=F32))
    gla = o * lax.rsqrt(ms + RMS_EPS) * gn_ref[...] * jax.nn.silu(gg_ref[...])
    od = df_ref[...] + db_ref[...]
    parts = []
    for hd in range(GDN_HEADS):
        seg = od[:, hd * GDN_DV:(hd + 1) * GDN_DV]
        parts.append(seg * lax.rsqrt(jnp.mean(seg * seg, axis=-1, keepdims=True) + RMS_EPS))
    gdn = jnp.concatenate(parts, 1) * dn_ref[...] * jax.nn.silu(dz_ref[...])
    a = (jnp.dot(gla.astype(BF16), w_ref[0:GLA_V_W, :], preferred_element_type=F32)
         + jnp.dot(na_ref[...], w_ref[GLA_V_W:GLA_V_W + NA_W, :], preferred_element_type=F32)
         + jnp.dot(gdn.astype(BF16), w_ref[GLA_V_W + NA_W:, :], preferred_element_type=F32))
    y = DN_ALPHA * x_ref[...] + mod_ref[0, 0:1, :] * a
    xn = _ln_rows(y) * ln_ref[0:1, :] + ln_ref[1:2, :]
    xo_ref[...] = xn
    h = _ln_rows(xn) * mod_ref[0, 2:3, :] + mod_ref[0, 1:2, :]
    h_ref[...] = h
    lg_ref[...] = jnp.dot(h, wr_ref[...], preferred_element_type=F32, precision=_HI) + br_ref[...]


def _outproj(n, gla_f, gla_b, gla_cols, gla_norm, na, gdn_f, gdn_b, gdn_cols, gdn_norm, x, mod, ln, w_bf,
             w_router_pad, b_router_pad, rows_per_group):
    d = x.shape[1]
    n_groups = mod.shape[0]
    row = lambda i: (i, 0)
    const = lambda i: (0, 0)
    return pl.pallas_call(
        _outproj_body,
        out_shape=(jax.ShapeDtypeStruct((n, d), F32), jax.ShapeDtypeStruct((n, d), F32),
                   jax.ShapeDtypeStruct((n, LANES), F32)),
        grid=(n // ROW_TILE,),
        in_specs=[
            pl.BlockSpec((ROW_TILE, GLA_V_W), row),
            pl.BlockSpec((ROW_TILE, GLA_V_W), row),
            pl.BlockSpec((ROW_TILE, GLA_V_W), lambda i: (i, (2 * GLA_QK_W + GLA_V_W) // GLA_V_W)),
            pl.BlockSpec((1, GLA_V_W), const),
            pl.BlockSpec((ROW_TILE, NA_W), row),
            pl.BlockSpec((ROW_TILE, GDN_V_W), row),
            pl.BlockSpec((ROW_TILE, GDN_V_W), row),
            pl.BlockSpec((ROW_TILE, GDN_V_W), lambda i: (i, (2 * GDN_QK_W + GDN_V_W) // GDN_V_W)),
            pl.BlockSpec((1, GDN_V_W), const),
            pl.BlockSpec((ROW_TILE, d), row),
            pl.BlockSpec((1, 3, d), lambda i: (_group_of_block(i, rows_per_group, n_groups), 0, 0)),
            pl.BlockSpec((2, d), const),
            pl.BlockSpec((MIX_W, d), const),
            pl.BlockSpec((d, LANES), const),
            pl.BlockSpec((1, LANES), const),
        ],
        out_specs=(pl.BlockSpec((ROW_TILE, d), row), pl.BlockSpec((ROW_TILE, d), row),
                   pl.BlockSpec((ROW_TILE, LANES), row)),
        compiler_params=pltpu.CompilerParams(dimension_semantics=("arbitrary",),
                                             vmem_limit_bytes=VMEM_LIMIT_BYTES),
        name="outproj",
    )(gla_f, gla_b, gla_cols, gla_norm, na, gdn_f, gdn_b, gdn_cols, gdn_norm, x, mod, ln, w_bf, w_router_pad,
      b_router_pad)


def _moe_body(be_ref, nv_ref, x_ref, wgu_ref, bgu_ref, wd_ref, bd_ref, o_ref, wgu_bf, wd_bf):
    i = pl.program_id(0)
    e = be_ref[i]
    e_prev = be_ref[jnp.maximum(i - 1, 0)]
    valid = i < nv_ref[0]

    @pl.when(valid & ((i == 0) | (e != e_prev)))
    def _():
        wgu_bf[...] = wgu_ref[0, 0].astype(BF16)
        wd_bf[...] = wd_ref[0, 0].astype(BF16)

    @pl.when(valid)
    def _():
        x = x_ref[...].astype(BF16)
        acc = jnp.zeros((MOE_BM, D_MODEL), F32) + bd_ref[0, 0]
        for c in range(D_EXPERT // F_CHUNK):
            lo = c * F_CHUNK
            gt = jnp.dot(x, wgu_bf[:, lo:lo + F_CHUNK], preferred_element_type=F32)
            gt = gt + bgu_ref[0, 0, :, lo:lo + F_CHUNK]
            up = jnp.dot(x, wgu_bf[:, D_EXPERT + lo:D_EXPERT + lo + F_CHUNK], preferred_element_type=F32)
            up = up + bgu_ref[0, 0, :, D_EXPERT + lo:D_EXPERT + lo + F_CHUNK]
            gt = jnp.minimum(gt, SWIGLU_LIMIT)
            up = jnp.clip(up, -SWIGLU_LIMIT, SWIGLU_LIMIT)
            act = (up + 1.0) * gt * jax.nn.sigmoid(SWIGLU_ALPHA * gt)
            acc = acc + jnp.dot(act.astype(BF16), wd_bf[lo:lo + F_CHUNK, :], preferred_element_type=F32)
        o_ref[...] = acc

    @pl.when(jnp.logical_not(valid))
    def _():
        o_ref[...] = jnp.zeros_like(o_ref)


def _moe_experts(layer, block_e, n_valid, xs, w_gate_up, b_gate_up, w_down, b_down):
    n_rows, d = xs.shape
    n_blocks = n_rows // MOE_BM
    f2 = 2 * D_EXPERT

    def blk(i, be, nv):
        return jnp.minimum(i, nv[0] - 1)

    grid_spec = pltpu.PrefetchScalarGridSpec(
        num_scalar_prefetch=2,
        grid=(n_blocks,),
        in_specs=[
            pl.BlockSpec((MOE_BM, d), lambda i, be, nv: (blk(i, be, nv), 0)),
            pl.BlockSpec((1, 1, d, f2), lambda i, be, nv: (layer, be[blk(i, be, nv)], 0, 0)),
            pl.BlockSpec((1, 1, 1, f2), lambda i, be, nv: (layer, be[blk(i, be, nv)], 0, 0)),
            pl.BlockSpec((1, 1, D_EXPERT, d), lambda i, be, nv: (layer, be[blk(i, be, nv)], 0, 0)),
            pl.BlockSpec((1, 1, 1, d), lambda i, be, nv: (layer, be[blk(i, be, nv)], 0, 0)),
        ],
        out_specs=pl.BlockSpec((MOE_BM, d), lambda i, be, nv: (i, 0)),
        scratch_shapes=[pltpu.VMEM((d, f2), BF16), pltpu.VMEM((D_EXPERT, d), BF16)],
    )
    return pl.pallas_call(
        _moe_body,
        out_shape=jax.ShapeDtypeStruct((n_rows, d), F32),
        grid_spec=grid_spec,
        compiler_params=pltpu.CompilerParams(dimension_semantics=("arbitrary",),
                                             vmem_limit_bytes=VMEM_LIMIT_BYTES),
        name="moe_experts",
    )(block_e, n_valid, xs, w_gate_up, b_gate_up.reshape(DEPTH, N_EXPERTS, 1, f2), w_down,
      b_down.reshape(DEPTH, N_EXPERTS, 1, d))


GATHER_ROWS = 1024


def _gather_body(group, idx_ref, src_ref, o_ref, sem):
    d = src_ref.shape[1]

    def row_copy(src_row, r):
        dst = o_ref.at[pl.ds(r // group, 1), pl.ds((r % group) * d, d)]
        return pltpu.make_async_copy(src_ref.at[pl.ds(src_row, 1)], dst, sem.at[0])

    for r in range(GATHER_ROWS):
        row_copy(idx_ref[0, 0, r], r).start(priority=r % 2)
    for r in range(GATHER_ROWS):
        row_copy(0, r).wait()


def _gather_rows(src, idx, group=1):
    n_out = idx.shape[0]
    n_steps = n_out // GATHER_ROWS
    d = src.shape[1]
    return pl.pallas_call(
        functools.partial(_gather_body, group),
        out_shape=jax.ShapeDtypeStruct((n_out // group, group * d), src.dtype),
        grid=(n_steps,),
        in_specs=[
            pl.BlockSpec((1, 1, GATHER_ROWS), lambda i: (i, 0, 0), memory_space=pltpu.SMEM),
            pl.BlockSpec(memory_space=pl.ANY),
        ],
        out_specs=pl.BlockSpec((GATHER_ROWS // group, group * d), lambda i: (i, 0)),
        scratch_shapes=[pltpu.SemaphoreType.DMA((1,))],
        compiler_params=pltpu.CompilerParams(dimension_semantics=("arbitrary",)),
        name="gather_rows",
    )(idx.reshape(n_steps, 1, GATHER_ROWS), src)


def _combine_body(y_ref, g_ref, x_ref, mod_ref, ln_ref, o_ref):
    d = x_ref.shape[1]
    g = g_ref[...]
    f = g[:, 0:1] * y_ref[:, 0:d]
    for k in range(1, TOP_K):
        f = f + g[:, k:k + 1] * y_ref[:, k * d:(k + 1) * d]
    y = DN_ALPHA * x_ref[...] + mod_ref[0, 0:1, :] * f
    o_ref[...] = _ln_rows(y) * ln_ref[0:1, :] + ln_ref[1:2, :]


def _combine(y4, gate, x1, gate2, ln, rows_per_group):
    n, d = x1.shape
    n_groups = gate2.shape[0]
    row = lambda i: (i, 0)
    return pl.pallas_call(
        _combine_body,
        out_shape=jax.ShapeDtypeStruct((n, d), F32),
        grid=(n // ROW_TILE,),
        in_specs=[
            pl.BlockSpec((ROW_TILE, TOP_K * d), row),
            pl.BlockSpec((ROW_TILE, TOP_K), row),
            pl.BlockSpec((ROW_TILE, d), row),
            pl.BlockSpec((1, 1, d), lambda i: (_group_of_block(i, rows_per_group, n_groups), 0, 0)),
            pl.BlockSpec((2, d), lambda i: (0, 0)),
        ],
        out_specs=pl.BlockSpec((ROW_TILE, d), row),
        compiler_params=pltpu.CompilerParams(dimension_semantics=("arbitrary",),
                                             vmem_limit_bytes=VMEM_LIMIT_BYTES),
        name="combine",
    )(y4, gate, x1, gate2, ln)


def _moe(layer, h, logits, w_gate_up, b_gate_up, w_down, b_down):
    n_tok, d = h.shape
    top_logit, top_e = lax.top_k(logits, TOP_K)
    gate = jax.nn.softmax(top_logit, axis=-1)
    nk = n_tok * TOP_K
    flat_e = top_e.reshape(-1).astype(jnp.int32)
    onehot = (flat_e[:, None] == jnp.arange(N_EXPERTS, dtype=jnp.int32)[None, :]).astype(jnp.int32)
    csum = jnp.cumsum(onehot, axis=0)
    counts = csum[-1]
    rank = jnp.sum((csum - 1) * onehot, axis=1)
    padded = (counts + MOE_BM - 1) // MOE_BM * MOE_BM
    pad_end = jnp.cumsum(padded)
    pad_start = pad_end - padded
    dest = jnp.sum(onehot * pad_start[None, :], axis=1) + rank
    n_blocks = nk // MOE_BM + N_EXPERTS
    flat_tok = jnp.arange(nk, dtype=jnp.int32) // TOP_K
    buf_tok = jnp.zeros((n_blocks * MOE_BM,), jnp.int32).at[dest].set(flat_tok)
    block_start = jnp.arange(n_blocks, dtype=jnp.int32) * MOE_BM
    block_e = jnp.minimum(jnp.searchsorted(pad_end, block_start, side='right'), N_EXPERTS - 1).astype(jnp.int32)
    n_valid = (pad_end[-1] // MOE_BM).astype(jnp.int32).reshape(1)
    xs = _gather_rows(h, buf_tok)
    yb = _moe_experts(layer, block_e, n_valid, xs, w_gate_up, b_gate_up, w_down, b_down)
    return _gather_rows(yb, dest, group=TOP_K), gate


NA_ROWS_PER_STEP = 4
NA_PATTERN_ROWS = (0, 1, 2, 3, GRID_W // 2, GRID_W - 3, GRID_W - 2, GRID_W - 1)


def _na_bias_table(rpb):
    rows = GRID_W
    pat_r = np.array(NA_PATTERN_ROWS)
    r0 = np.clip(pat_r - NA_WIN_R // 2, 0, rows - NA_WIN_R)
    dr = r0[:, None] + np.arange(NA_WIN_R)[None, :] - pat_r[:, None]
    cidx = np.arange(GRID_W)
    c0 = np.clip(cidx - NA_WIN_C // 2, 0, GRID_W - NA_WIN_C)
    col_ok = (cidx[None, :] >= c0[:, None]) & (cidx[None, :] < c0[:, None] + NA_WIN_C)
    dc = np.clip(cidx[None, :] - cidx[:, None] + NA_WIN_C - 1, 0, 2 * NA_WIN_C - 2)
    n_dc = 2 * NA_WIN_C - 1
    sel = jnp.stack([jnp.stack([rpb[:, int(dr[p, i]) + NA_WIN_R - 1, :] for i in range(NA_WIN_R)], 1)
                     for p in range(len(NA_PATTERN_ROWS))], 0)
    onehot = jnp.asarray((dc[:, :, None] == np.arange(n_dc)[None, None, :]).astype(np.float32))
    bias = jnp.einsum('phic,qkc->phqik', sel, onehot, precision=_HI)
    bias = jnp.where(col_ok[None, None, :, None, :], bias, -jnp.inf)
    return bias.reshape(len(NA_PATTERN_ROWS), NA_HEADS * GRID_W, NA_WIN_R * GRID_W)


def _head_block_mask(n_q):
    shape = (NA_HEADS * n_q, NA_W)
    return (lax.broadcasted_iota(jnp.int32, shape, 0) // n_q) == (lax.broadcasted_iota(jnp.int32, shape, 1) // NA_HD)


def _heads_on_rows(q, mask):
    qt = jnp.concatenate([q] * NA_HEADS, axis=0)
    return jnp.where(mask, qt, jnp.zeros_like(qt)) * (NA_HD ** -0.5)


def _heads_to_lanes(o_all, n_q, n_heads, head_w):
    lane_h = lax.broadcasted_iota(jnp.int32, (n_q, n_heads * head_w), 1) // head_w
    out = jnp.zeros((n_q, n_heads * head_w), F32)
    for h in range(n_heads):
        out = out + jnp.where(lane_h == h, o_all[h * n_q:(h + 1) * n_q], 0.0)
    return out


def _na_body(q_ref, k_ref, v_ref, kc_ref, vc_ref, bias_ref, o_ref):
    g = pl.program_id(1)
    mask = _head_block_mask(GRID_W)
    kc = kc_ref[...]
    vc = vc_ref[...]
    rows = GRID_W
    for i in range(NA_ROWS_PER_STEP):
        r = g * NA_ROWS_PER_STEP + i
        r0 = jnp.clip(r - NA_WIN_R // 2, 0, rows - NA_WIN_R)
        pat = jnp.where(r < 4, r, jnp.where(r > rows - 4, r - (rows - 8), 4))
        qb = _heads_on_rows(q_ref[i * GRID_W:(i + 1) * GRID_W, :], mask)
        start = pl.multiple_of(r0 * GRID_W, GRID_W)
        ks = k_ref[pl.ds(start, NA_WIN_R * GRID_W), :]
        vs = v_ref[pl.ds(start, NA_WIN_R * GRID_W), :]
        s_loc = lax.dot_general(qb, ks, _NT, preferred_element_type=F32) + bias_ref[pat]
        s_ctx = lax.dot_general(qb, kc, _NT, preferred_element_type=F32)
        m = jnp.maximum(jnp.max(s_loc, axis=-1, keepdims=True), jnp.max(s_ctx, axis=-1, keepdims=True))
        p_loc = jnp.exp(s_loc - m)
        p_ctx = jnp.exp(s_ctx - m)
        denom = jnp.sum(p_loc, axis=-1, keepdims=True) + jnp.sum(p_ctx, axis=-1, keepdims=True)
        o_all = (jnp.dot(p_loc.astype(BF16), vs, preferred_element_type=F32)
                 + jnp.dot(p_ctx.astype(BF16), vc, preferred_element_type=F32)) / denom
        o_ref[i * GRID_W:(i + 1) * GRID_W, :] = _heads_to_lanes(o_all, GRID_W, NA_HEADS, NA_HD).astype(o_ref.dtype)


def _na_latent(qkv, bias, b_, S, L):
    tq = NA_ROWS_PER_STEP * GRID_W
    steps = S // tq
    ctx_blk0 = b_ * S // L
    return pl.pallas_call(
        _na_body,
        out_shape=jax.ShapeDtypeStruct((b_ * S, NA_W), BF16),
        grid=(b_, steps),
        in_specs=[
            pl.BlockSpec((tq, NA_W), lambda b, g: (b * steps + g, 0)),
            pl.BlockSpec((S, NA_W), lambda b, g: (b, 1)),
            pl.BlockSpec((S, NA_W), lambda b, g: (b, 2)),
            pl.BlockSpec((L, NA_W), lambda b, g: (ctx_blk0 + b, 1)),
            pl.BlockSpec((L, NA_W), lambda b, g: (ctx_blk0 + b, 2)),
            pl.BlockSpec(bias.shape, lambda b, g: (0, 0, 0)),
        ],
        out_specs=pl.BlockSpec((tq, NA_W), lambda b, g: (b * steps + g, 0)),
        compiler_params=pltpu.CompilerParams(dimension_semantics=("arbitrary", "arbitrary"),
                                             vmem_limit_bytes=VMEM_LIMIT_BYTES),
        name="na_latent",
    )(qkv, qkv, qkv, qkv, qkv, bias)


def _na_ctx_body(q_ref, k_ref, v_ref, o_ref):
    n_q = q_ref.shape[0]
    qb = _heads_on_rows(q_ref[...], _head_block_mask(n_q))
    s = lax.dot_general(qb, k_ref[...], _NT, preferred_element_type=F32)
    p = jnp.exp(s - jnp.max(s, axis=-1, keepdims=True))
    o_all = jnp.dot(p.astype(BF16), v_ref[...], preferred_element_type=F32) / jnp.sum(p, axis=-1, keepdims=True)
    o_ref[...] = _heads_to_lanes(o_all, n_q, NA_HEADS, NA_HD).astype(o_ref.dtype)


def _na_context(qkv, b_, S, L):
    ctx_blk0 = b_ * S // L
    return pl.pallas_call(
        _na_ctx_body,
        out_shape=jax.ShapeDtypeStruct((b_ * L, NA_W), BF16),
        grid=(b_,),
        in_specs=[pl.BlockSpec((L, NA_W), lambda b: (ctx_blk0 + b, 0)),
                  pl.BlockSpec((L, NA_W), lambda b: (ctx_blk0 + b, 1)),
                  pl.BlockSpec((L, NA_W), lambda b: (ctx_blk0 + b, 2))],
        out_specs=pl.BlockSpec((L, NA_W), lambda b: (b, 0)),
        compiler_params=pltpu.CompilerParams(dimension_semantics=("arbitrary",)),
        name="na_context",
    )(qkv, qkv, qkv)


GLA_BLOCK = 256
GLA_SUB = 8


def _rope_lane_tables(S, L):
    half = GLA_DK // 2
    inv_freq = ROPE_BASE ** (-jnp.arange(0, half, 2, dtype=F32) / half)
    t = jnp.arange(S)
    d = np.arange(GLA_QK_W) % GLA_DK
    use_col = d >= half
    fidx = d % (half // 2)
    sign = np.where((d % half) < half // 2, -1.0, 1.0).astype(np.float32)
    pos = jnp.where(use_col[None, :], (t % GRID_W)[:, None], (t // GRID_W)[:, None]).astype(F32)
    ang = pos * inv_freq[fidx][None, :]
    cos = jnp.concatenate([jnp.cos(ang), jnp.ones((L, GLA_QK_W), F32)], 0)
    sin = jnp.concatenate([jnp.sin(ang) * sign[None, :], jnp.zeros((L, GLA_QK_W), F32)], 0)
    return cos, sin


def _rope_lanes(x, cos, sin):
    pair = GLA_DK // 4
    lane = lax.broadcasted_iota(jnp.int32, x.shape, 1)
    first = (lane % (2 * pair)) < pair
    partner = jnp.where(first, pltpu.roll(x, LANES - pair, 1), pltpu.roll(x, pair, 1))
    return x * cos + partner * sin


def _bcast_rows(x, r, n):
    return jnp.broadcast_to(x[r:r + 1, :], (n, x.shape[1]))


def _gla_consts(rev):
    C = GLA_CHUNK
    r = lax.broadcasted_iota(jnp.int32, (C, C), 0)
    c = lax.broadcasted_iota(jnp.int32, (C, C), 1)
    tri = jnp.where((c >= r) if rev else (c <= r), 1.0, 0.0).astype(F32)
    n_exp = (C // GLA_SUB - 1) * GLA_QK_W
    head_mask_q = ((lax.broadcasted_iota(jnp.int32, (GLA_HEADS * C, n_exp), 0) // C)
                   == ((lax.broadcasted_iota(jnp.int32, (GLA_HEADS * C, n_exp), 1) % GLA_QK_W) // GLA_DK))
    s3 = ((lax.broadcasted_iota(jnp.int32, (GLA_QK_W, GLA_V_W), 0) // GLA_DK)
          == (lax.broadcasted_iota(jnp.int32, (GLA_QK_W, GLA_V_W), 1) // GLA_DV)).astype(BF16)
    n_p = C * GLA_SUB
    rsum = (lax.broadcasted_iota(jnp.int32, (C, n_p), 0)
            == lax.broadcasted_iota(jnp.int32, (C, n_p), 1) // GLA_SUB).astype(BF16)
    row = lax.broadcasted_iota(jnp.int32, (n_p, GLA_QK_W), 0)
    t_loc, s_loc = (row // GLA_SUB) % GLA_SUB, row % GLA_SUB
    diag_ok = (s_loc >= t_loc) if rev else (s_loc <= t_loc)
    st_mask = ((lax.broadcasted_iota(jnp.int32, (GLA_V_W, GLA_QK_W), 0) // GLA_DV)
               == (lax.broadcasted_iota(jnp.int32, (GLA_V_W, GLA_QK_W), 1) // GLA_DK))
    return tri, head_mask_q, s3, rsum, diag_ok, st_mask


def _gla_chunk(q, k, v, la, st, rev, consts):
    tri, head_mask_q, s3, rsum, diag_ok, st_mask = consts
    C, nb = GLA_CHUNK, GLA_CHUNK // GLA_SUB
    gc = jnp.dot(tri, la, precision=_HI, preferred_element_type=F32)
    zeros = jnp.zeros((GLA_SUB, GLA_QK_W), F32)
    if not rev:
        g_tot = gc[C - 1:C]
        ref_rows = [None] + [GLA_SUB * i - 1 for i in range(1, nb)]
        blk_rows = [GLA_SUB * j + GLA_SUB - 1 for j in range(nb)]
    else:
        g_tot = gc[0:1]
        ref_rows = [GLA_SUB * (i + 1) for i in range(nb - 1)] + [None]
        blk_rows = [GLA_SUB * j for j in range(nb)]
    g_ref = jnp.concatenate([zeros if r is None else _bcast_rows(gc, r, GLA_SUB) for r in ref_rows], 0)
    g_blk = jnp.concatenate([_bcast_rows(gc, r, GLA_SUB) for r in blk_rows], 0)
    qt = q * jnp.exp(gc - g_ref)
    kh = k * jnp.exp(g_blk - gc)
    sub = lax.broadcasted_iota(jnp.int32, (C, GLA_QK_W), 0) // GLA_SUB
    k_slabs, q_slabs = [], []
    for i in range(nb):
        if ref_rows[i] is None:
            continue
        keys_ok = (sub > i) if rev else (sub < i)
        between = jnp.minimum(gc[ref_rows[i]:ref_rows[i] + 1] - g_blk, 0.0)
        k_slabs.append(jnp.where(keys_ok, kh * jnp.exp(between), 0.0))
        q_slabs.append(jnp.where(sub == i, qt, 0.0))
    k_exp = jnp.concatenate(k_slabs, 1).astype(BF16)
    q_exp = jnp.concatenate(q_slabs, 1)
    q_exp = jnp.where(head_mask_q, jnp.concatenate([q_exp] * GLA_HEADS, 0), 0.0).astype(BF16)
    attn_off = lax.dot_general(q_exp, k_exp, _NT, preferred_element_type=F32)
    vb = v.astype(BF16)
    o_off_all = jnp.dot(attn_off.astype(BF16), vb, preferred_element_type=F32)
    o = _heads_to_lanes(o_off_all, C, GLA_HEADS, GLA_DV)
    pieces = []
    for i in range(nb):
        lo = i * GLA_SUB
        k_i, g_i = k[lo:lo + GLA_SUB], gc[lo:lo + GLA_SUB]
        for t in range(GLA_SUB):
            e = jnp.exp(jnp.minimum(_bcast_rows(gc, lo + t, GLA_SUB) - g_i, 0.0))
            pieces.append(_bcast_rows(q, lo + t, GLA_SUB) * k_i * e)
    p = jnp.where(diag_ok, jnp.concatenate(pieces, 0), 0.0).astype(BF16)
    a2 = jnp.dot(p, s3, preferred_element_type=F32)
    v_rep = jnp.concatenate([v[i * GLA_SUB:(i + 1) * GLA_SUB] for i in range(nb) for _ in range(GLA_SUB)], 0)
    o = o + jnp.dot(rsum, (a2 * v_rep).astype(BF16), preferred_element_type=F32)
    q_dec = (q * jnp.exp(gc)).astype(BF16)
    k_dec = (k * jnp.exp(g_tot - gc)).astype(BF16)
    o = o + lax.dot_general(q_dec, st.astype(BF16), _NT, preferred_element_type=F32)
    upd = lax.dot_general(vb, k_dec, _TN, preferred_element_type=F32)
    return o, st * jnp.exp(g_tot) + jnp.where(st_mask, upd, 0.0)


def _gla_body(xf_ref, xb_ref, cf_ref, sf_ref, cb_ref, sb_ref, w2_ref, b2_ref, of_ref, ob_ref,
              q_s, k_s, la_s, st_s):
    @pl.when(pl.program_id(1) == 0)
    def _():
        st_s[...] = jnp.zeros_like(st_s)

    gate_lo = 2 * GLA_QK_W + 2 * GLA_V_W
    for d, (x_ref, cos_ref, sin_ref) in enumerate(((xf_ref, cf_ref, sf_ref), (xb_ref, cb_ref, sb_ref))):
        z = jnp.dot(x_ref[:, gate_lo:], w2_ref[...], precision=_HI, preferred_element_type=F32) + b2_ref[...]
        la_s[d] = jax.nn.log_sigmoid(z[:, d * GLA_QK_W:(d + 1) * GLA_QK_W]) * (1.0 / GLA_TAU)
        q_s[d] = _rope_lanes(x_ref[:, 0:GLA_QK_W], cos_ref[...], sin_ref[...]) * (GLA_DK ** -0.5)
        k_s[d] = _rope_lanes(x_ref[:, GLA_QK_W:2 * GLA_QK_W], cos_ref[...], sin_ref[...])

    consts = (_gla_consts(False), _gla_consts(True))
    n_chunks = GLA_BLOCK // GLA_CHUNK

    def step(i, carry):
        starts = (i * GLA_CHUNK, (n_chunks - 1 - i) * GLA_CHUNK)
        for d, (x_ref, o_ref) in enumerate(((xf_ref, of_ref), (xb_ref, ob_ref))):
            rows = pl.ds(pl.multiple_of(starts[d], GLA_CHUNK), GLA_CHUNK)
            v = x_ref[rows, 2 * GLA_QK_W:2 * GLA_QK_W + GLA_V_W]
            o, st = _gla_chunk(q_s[d, rows, :], k_s[d, rows, :], v, la_s[d, rows, :], st_s[d], d == 1, consts[d])
            o_ref[rows, :] = o
            st_s[d] = st
        return carry

    lax.fori_loop(0, n_chunks, step, 0, unroll=2)


def _gla_gate_weights(w_a2, b_a):
    w2 = jnp.zeros((LANES, 2 * GLA_QK_W), F32)
    w2 = w2.at[0:GLA_RANK, 0:GLA_QK_W].set(w_a2[0]).at[GLA_RANK:2 * GLA_RANK, GLA_QK_W:].set(w_a2[1])
    return w2, b_a.reshape(1, 2 * GLA_QK_W)


def _gla_scan(cols, cos, sin, w2, b2, b_, S, L):
    n = cols.shape[0]
    lat_blocks = S // GLA_BLOCK
    ctx0 = b_ * S // GLA_BLOCK
    assert L == GLA_BLOCK
    fwd = lambda b, j: jnp.where(j == 0, ctx0 + b, b * lat_blocks + j - 1)
    bwd = lambda b, j: jnp.where(j == 0, ctx0 + b, b * lat_blocks + lat_blocks - j)
    fwd_t = lambda b, j: jnp.where(j == 0, lat_blocks, j - 1)
    bwd_t = lambda b, j: jnp.where(j == 0, lat_blocks, lat_blocks - j)
    blk = lambda f: (lambda b, j: (f(b, j), 0))
    const = lambda b, j: (0, 0)
    return pl.pallas_call(
        _gla_body,
        out_shape=(jax.ShapeDtypeStruct((n, GLA_V_W), F32), jax.ShapeDtypeStruct((n, GLA_V_W), F32)),
        grid=(b_, lat_blocks + 1),
        in_specs=[
            pl.BlockSpec((GLA_BLOCK, GLA_SEG_W), blk(fwd)),
            pl.BlockSpec((GLA_BLOCK, GLA_SEG_W), blk(bwd)),
            pl.BlockSpec((GLA_BLOCK, GLA_QK_W), blk(fwd_t)),
            pl.BlockSpec((GLA_BLOCK, GLA_QK_W), blk(fwd_t)),
            pl.BlockSpec((GLA_BLOCK, GLA_QK_W), blk(bwd_t)),
            pl.BlockSpec((GLA_BLOCK, GLA_QK_W), blk(bwd_t)),
            pl.BlockSpec((LANES, 2 * GLA_QK_W), const),
            pl.BlockSpec((1, 2 * GLA_QK_W), const),
        ],
        out_specs=(pl.BlockSpec((GLA_BLOCK, GLA_V_W), blk(fwd)), pl.BlockSpec((GLA_BLOCK, GLA_V_W), blk(bwd))),
        scratch_shapes=[pltpu.VMEM((2, GLA_BLOCK, GLA_QK_W), F32), pltpu.VMEM((2, GLA_BLOCK, GLA_QK_W), F32),
                        pltpu.VMEM((2, GLA_BLOCK, GLA_QK_W), F32), pltpu.VMEM((2, GLA_V_W, GLA_QK_W), F32)],
        compiler_params=pltpu.CompilerParams(dimension_semantics=("arbitrary", "arbitrary"),
                                             vmem_limit_bytes=VMEM_LIMIT_BYTES),
        name="gla_scan",
    )(cols, cols, cos, sin, cos, sin, w2, b2)


SUBLANES = 8
GDN_CONV_W = 2 * GDN_QK_W + GDN_V_W
GDN_GATE_LO = 2 * GDN_QK_W + 2 * GDN_V_W
GDN_BLOCK = 256
GDN_HT = GDN_HEADS * GDN_CHUNK


def _stack_heads(x):
    return jnp.concatenate([x[:, h * GDN_DK:(h + 1) * GDN_DK] for h in range(GDN_HEADS)], 0)


def _stack_cols(a, lane0, width):
    return jnp.concatenate([jnp.broadcast_to(a[:, lane0 + h:lane0 + h + 1], (a.shape[0], width))
                            for h in range(GDN_HEADS)], 0)


def _block_diag(xs, mask):
    return jnp.where(mask, jnp.concatenate([xs] * GDN_HEADS, 1), 0.0).astype(BF16)


def _gdn_consts(rev):
    C, HT = GDN_CHUNK, GDN_HT
    rr = lax.broadcasted_iota(jnp.int32, (HT, HT), 0)
    cc = lax.broadcasted_iota(jnp.int32, (HT, HT), 1)
    same_head = (rr // C) == (cc // C)
    t, s = rr % C, cc % C
    strict = same_head & ((s > t) if rev else (s < t))
    incl = same_head & ((s >= t) if rev else (s <= t))
    bd_mask = ((lax.broadcasted_iota(jnp.int32, (HT, GDN_QK_W), 0) // C)
               == (lax.broadcasted_iota(jnp.int32, (HT, GDN_QK_W), 1) // GDN_DK))
    return strict, incl, bd_mask


def _chunk_cumsum_matrix(n, chunk, rev):
    r = lax.broadcasted_iota(jnp.int32, (n, n), 0)
    c = lax.broadcasted_iota(jnp.int32, (n, n), 1)
    return jnp.where((r // chunk == c // chunk) & ((c >= r) if rev else (c <= r)), 1.0, 0.0).astype(F32)


def _gdn_chunk(qs, ks, vs, gc, g_lane0, bcol, st, rev, consts):
    strict, incl, bd_mask = consts
    C = GDN_CHUNK
    gcol = _stack_cols(gc, g_lane0, GDN_HT)
    g_t = gc.T
    g_row = jnp.concatenate([g_t[g_lane0 + h:g_lane0 + h + 1, :] for h in range(GDN_HEADS)], 1)
    dec = jnp.exp(jnp.minimum(gcol - g_row, 0.0))
    HT = GDN_HT
    kb = ks * bcol
    kq = jnp.concatenate([_block_diag(kb, bd_mask), _block_diag(qs, bd_mask)], 0)
    nq = lax.dot_general(kq, _block_diag(ks, bd_mask), _NT, preferred_element_type=F32)
    n = jnp.where(strict, nq[:HT] * dec, 0.0)
    qk = jnp.where(incl, nq[HT:] * dec, 0.0)
    tp = -n
    nb = n.astype(BF16)
    p = jnp.dot(nb, nb, preferred_element_type=F32)
    n_factors = int(np.log2(C)) - 1
    for it in range(n_factors):
        pb = p.astype(BF16)
        if it < n_factors - 1:
            both = jnp.dot(jnp.concatenate([tp.astype(BF16), pb], 0), pb, preferred_element_type=F32)
            tp, p = tp + p + both[:HT], both[HT:]
        else:
            tp = tp + p + jnp.dot(tp.astype(BF16), pb, preferred_element_type=F32)
    g = gcol[:, :GDN_DK]
    last = 0 if rev else C - 1
    g_last = jnp.concatenate([_bcast_rows(g, h * C + last, C) for h in range(GDN_HEADS)], 0)
    eg = jnp.exp(g)
    q_dec = qs * eg
    kbe = kb * eg
    k_dec = ks * jnp.exp(g_last - g)
    from_state = jnp.dot(jnp.concatenate([_block_diag(kbe, bd_mask), _block_diag(q_dec, bd_mask)], 0),
                         st.astype(BF16), preferred_element_type=F32)
    r = vs * bcol - from_state[:HT]
    v_new = r + jnp.dot(tp.astype(BF16), r.astype(BF16), preferred_element_type=F32)
    vnb = v_new.astype(BF16)
    o = from_state[HT:] + jnp.dot(qk.astype(BF16), vnb, preferred_element_type=F32)
    a_last = jnp.exp(jnp.concatenate([g_last[h * C:(h + 1) * C] for h in range(GDN_HEADS)
                                      for _ in range(GDN_DK // C)], 0))
    st_new = st * a_last + lax.dot_general(_block_diag(k_dec, bd_mask), vnb, _TN, preferred_element_type=F32)
    return o, st_new


def _gdn_prep(x_ref, prev_ref, next_ref, has_prev, has_next, w_ref, nega_ref, dt_ref):
    pad = GDN_CONV // 2
    cur = x_ref[:, 0:GDN_CONV_W]
    before = jnp.where(has_prev, prev_ref[SUBLANES - pad:SUBLANES, 0:GDN_CONV_W], 0.0)
    after = jnp.where(has_next, next_ref[0:pad, 0:GDN_CONV_W], 0.0)
    xe = jnp.concatenate([before, cur, after], 0)
    y = w_ref[0:1, :] * xe[0:GDN_BLOCK]
    for j in range(1, GDN_CONV):
        y = y + w_ref[j:j + 1, :] * xe[j:j + GDN_BLOCK]
    y = y * jax.nn.sigmoid(y)

    def l2n(t):
        parts = []
        for h in range(GDN_HEADS):
            seg = t[:, h * GDN_DK:(h + 1) * GDN_DK]
            parts.append(seg * lax.rsqrt(jnp.sum(seg * seg, -1, keepdims=True) + 1e-6))
        return jnp.concatenate(parts, 1)

    q = l2n(y[:, 0:GDN_QK_W]) * (GDN_DK ** -0.5)
    k = l2n(y[:, GDN_QK_W:2 * GDN_QK_W])
    v = y[:, 2 * GDN_QK_W:]
    tail = x_ref[:, GDN_GATE_LO:]
    beta = jax.nn.sigmoid(tail)
    tb = tail + dt_ref[...]
    softplus = jnp.maximum(tb, 0.0) + jnp.log(1.0 + jnp.exp(-jnp.abs(tb)))
    return q, k, v, beta, nega_ref[...] * softplus


def _gdn_body(lat_blocks, xf_ref, pf_ref, nf_ref, xb_ref, pb_ref, nb_ref, w_ref, nega_ref, dt_ref, of_ref, ob_ref,
              q_s, k_s, v_s, b_s, g_s, st_s):
    j = pl.program_id(1)

    @pl.when(j == 0)
    def _():
        st_s[...] = jnp.zeros_like(st_s)

    t_blk = (j - 1, lat_blocks - j)
    for d, (x_ref, p_ref, n_ref) in enumerate(((xf_ref, pf_ref, nf_ref), (xb_ref, pb_ref, nb_ref))):
        has_prev = (j > 0) & (t_blk[d] > 0)
        has_next = (j > 0) & (t_blk[d] < lat_blocks - 1)
        q_s[d], k_s[d], v_s[d], b_s[d], g = _gdn_prep(x_ref, p_ref, n_ref, has_prev, has_next,
                                                      w_ref, nega_ref, dt_ref)
        g_s[d] = jnp.dot(_chunk_cumsum_matrix(GDN_BLOCK, GDN_CHUNK, d == 1), g, precision=_HI,
                         preferred_element_type=F32)

    consts = (_gdn_consts(False), _gdn_consts(True))
    n_chunks = GDN_BLOCK // GDN_CHUNK

    def step(i, carry):
        starts = (i * GDN_CHUNK, (n_chunks - 1 - i) * GDN_CHUNK)
        for d, o_ref in enumerate((of_ref, ob_ref)):
            rows = pl.ds(pl.multiple_of(starts[d], GDN_CHUNK), GDN_CHUNK)
            bcol = _stack_cols(b_s[d, rows, :], d * GDN_HEADS, GDN_DK)
            o, st = _gdn_chunk(_stack_heads(q_s[d, rows, :]), _stack_heads(k_s[d, rows, :]),
                               _stack_heads(v_s[d, rows, :]), g_s[d, rows, :], (2 + d) * GDN_HEADS, bcol,
                               st_s[d], d == 1, consts[d])
            o_ref[rows, :] = jnp.concatenate([o[h * GDN_CHUNK:(h + 1) * GDN_CHUNK] for h in range(GDN_HEADS)], 1)
            st_s[d] = st
        return carry

    lax.fori_loop(0, n_chunks, step, 0)


def _gdn_gate_consts(a_log, dt_bias):
    nega = jnp.zeros((1, LANES), F32).at[0, 2 * GDN_HEADS:4 * GDN_HEADS].set(-jnp.exp(a_log).reshape(-1))
    dt = jnp.zeros((1, LANES), F32).at[0, 2 * GDN_HEADS:4 * GDN_HEADS].set(dt_bias.reshape(-1))
    return nega, dt


def _gdn_scan(cols, conv_w, nega, dt, b_, S, L):
    n = cols.shape[0]
    lat_blocks = S // GDN_BLOCK
    ctx0 = b_ * S // GDN_BLOCK
    assert L == GDN_BLOCK
    per8 = GDN_BLOCK // SUBLANES
    n8 = n // SUBLANES
    fwd = lambda b, j: jnp.where(j == 0, ctx0 + b, b * lat_blocks + j - 1)
    bwd = lambda b, j: jnp.where(j == 0, ctx0 + b, b * lat_blocks + lat_blocks - j)
    blk = lambda f: (lambda b, j: (f(b, j), 0))
    prev8 = lambda f: (lambda b, j: (jnp.maximum(f(b, j) * per8 - 1, 0), 0))
    next8 = lambda f: (lambda b, j: (jnp.minimum(f(b, j) * per8 + per8, n8 - 1), 0))
    const = lambda b, j: (0, 0)
    x_spec = lambda f: pl.BlockSpec((GDN_BLOCK, GDN_SEG_W), blk(f))
    halo = lambda m: pl.BlockSpec((SUBLANES, GDN_SEG_W), m)
    return pl.pallas_call(
        functools.partial(_gdn_body, lat_blocks),
        out_shape=(jax.ShapeDtypeStruct((n, GDN_V_W), F32), jax.ShapeDtypeStruct((n, GDN_V_W), F32)),
        grid=(b_, lat_blocks + 1),
        in_specs=[x_spec(fwd), halo(prev8(fwd)), halo(next8(fwd)), x_spec(bwd), halo(prev8(bwd)), halo(next8(bwd)),
                  pl.BlockSpec((GDN_CONV, GDN_CONV_W), const), pl.BlockSpec((1, LANES), const),
                  pl.BlockSpec((1, LANES), const)],
        out_specs=(pl.BlockSpec((GDN_BLOCK, GDN_V_W), blk(fwd)), pl.BlockSpec((GDN_BLOCK, GDN_V_W), blk(bwd))),
        scratch_shapes=[pltpu.VMEM((2, GDN_BLOCK, GDN_QK_W), F32), pltpu.VMEM((2, GDN_BLOCK, GDN_QK_W), F32),
                        pltpu.VMEM((2, GDN_BLOCK, GDN_V_W), F32), pltpu.VMEM((2, GDN_BLOCK, LANES), F32),
                        pltpu.VMEM((2, GDN_BLOCK, LANES), F32), pltpu.VMEM((2, GDN_QK_W, GDN_DV), F32)],
        compiler_params=pltpu.CompilerParams(dimension_semantics=("arbitrary", "arbitrary"),
                                             vmem_limit_bytes=VMEM_LIMIT_BYTES),
        name="gdn_scan",
    )(cols, cols, cols, cols, cols, cols, conv_w, nega, dt)


def kernel(x, c, ctx, c_ctx, w_ada, b_ada, w_in, gla_w_a2, gla_b_a, gla_norm, na_rpb, gdn_conv, gdn_a_log,
           gdn_dt_bias, gdn_norm, w_out, ln1_g, ln1_b, w_router, b_router, w_gate_up, b_gate_up, w_down, b_down,
           ln2_g, ln2_b):
    b_, S, D = x.shape
    L = ctx.shape[1]
    n_lat, n_ctx = b_ * S, b_ * L
    rope_cos, rope_sin = _rope_lane_tables(S, L)
    sc = jnp.concatenate([jax.nn.silu(c), jax.nn.silu(c_ctx)[None, :]], 0)
    xt = jnp.concatenate([x.reshape(n_lat, D), ctx.reshape(n_ctx, D)], 0)
    for l in range(DEPTH):
        last = l == DEPTH - 1
        m = (jnp.dot(sc, w_ada[l], precision=_HI) + b_ada[l]).reshape(b_ + 1, N_MOD, D)
        shift1, scale1, gate1, shift2, scale2, gate2 = (m[:, j] for j in range(N_MOD))
        gla_cols, na_cols, gdn_cols = _inproj(xt, jnp.stack([shift1, 1.0 + scale1], 1), _inproj_weights(w_in[l]), S)
        gla_w2, gla_b2 = _gla_gate_weights(gla_w_a2[l], gla_b_a[l])
        gla_f, gla_b = _gla_scan(gla_cols, rope_cos, rope_sin, gla_w2, gla_b2, b_, S, L)
        na = jnp.concatenate([_na_latent(na_cols, _na_bias_table(na_rpb[l]), b_, S, L),
                              _na_context(na_cols, b_, S, L)], 0)
        gdn_nega, gdn_dt = _gdn_gate_consts(gdn_a_log[l], gdn_dt_bias[l])
        gdn_f, gdn_b = _gdn_scan(gdn_cols, gdn_conv[l], gdn_nega, gdn_dt, b_, S, L)
        n_rows = n_lat if last else n_lat + n_ctx
        w_router_pad = jnp.pad(w_router[l], ((0, 0), (0, LANES - N_EXPERTS)))
        b_router_pad = jnp.pad(b_router[l], (0, LANES - N_EXPERTS)).reshape(1, LANES)
        x1, h2, logits = _outproj(n_rows, gla_f, gla_b, gla_cols, jnp.tile(gla_norm[l], GLA_HEADS)[None, :], na,
                                  gdn_f, gdn_b, gdn_cols, jnp.tile(gdn_norm[l], GDN_HEADS)[None, :], xt,
                                  jnp.stack([gate1, shift2, 1.0 + scale2], 1),
                                  jnp.stack([ln1_g[l], ln1_b[l]], 0), w_out[l].astype(BF16),
                                  w_router_pad, b_router_pad, S)
        y4, gate = _moe(l, h2, logits[:, :N_EXPERTS], w_gate_up, b_gate_up, w_down, b_down)
        xt = _combine(y4, gate, x1, gate2[:, None, :], jnp.stack([ln2_g[l], ln2_b[l]], 0), S)
    return xt[:n_lat].reshape(b_, S, D)
```

```python
import functools

import numpy as np

import jax
import jax.numpy as jnp
from jax import lax
from jax.experimental import pallas as pl
from jax.experimental.pallas import tpu as pltpu

D_MODEL = 1024
DEPTH = 4
GRID_W = 64
GLA_HEADS, GLA_DK, GLA_DV, GLA_RANK, GLA_TAU, GLA_CHUNK = 4, 32, 64, 16, 16.0, 32
ROPE_BASE = 10000.0
NA_HEADS, NA_HD, NA_WIN_R, NA_WIN_C = 4, 64, 8, 16
GDN_HEADS, GDN_DK, GDN_DV, GDN_CONV, GDN_CHUNK = 4, 128, 128, 5, 64
N_EXPERTS, TOP_K, D_EXPERT = 32, 4, 1024
SWIGLU_LIMIT, SWIGLU_ALPHA = 7.0, 1.702
N_MOD = 6
LN_EPS, RMS_EPS = 1e-5, 1e-6
DN_ALPHA = (2 * DEPTH) ** 0.25

F32 = jnp.float32
BF16 = jnp.bfloat16

LANES = 128
VMEM_LIMIT_BYTES = 56 * 1024 * 1024
ROW_TILE = 512
MOE_BM = 512
F_CHUNK = 512

GLA_QK_W = GLA_HEADS * GLA_DK
GLA_V_W = GLA_HEADS * GLA_DV
NA_W = NA_HEADS * NA_HD
GDN_QK_W = GDN_HEADS * GDN_DK
GDN_V_W = GDN_HEADS * GDN_DV
GLA_SEG_W = 2 * GLA_QK_W + 2 * GLA_V_W + LANES
NA_SEG_W = 3 * NA_W
GDN_SEG_W = 2 * GDN_QK_W + 2 * GDN_V_W + LANES
D_IN_SEG = GLA_SEG_W + NA_SEG_W + GDN_SEG_W
MIX_W = GLA_V_W + NA_W + GDN_V_W

_NT = (((1,), (1,)), ((), ()))
_TN = (((0,), (0,)), ((), ()))
_HI = lax.Precision.HIGHEST


def _ln_rows(x):
    mu = jnp.mean(x, axis=-1, keepdims=True)
    xc = x - mu
    var = jnp.mean(xc * xc, axis=-1, keepdims=True)
    return xc * lax.rsqrt(var + LN_EPS)


def _group_of_block(i, rows_per_group, n_groups):
    return jnp.minimum(i * ROW_TILE // rows_per_group, n_groups - 1)


def _inproj_weights(w):
    o = np.cumsum((0, GLA_QK_W, GLA_QK_W, GLA_V_W, 2 * GLA_RANK, GLA_V_W, NA_W, NA_W, NA_W,
                   GDN_QK_W, GDN_QK_W, GDN_V_W, 2 * GDN_HEADS, 2 * GDN_HEADS, GDN_V_W)).tolist()
    col = lambda i: w[:, o[i]:o[i + 1]]
    zeros = lambda n: jnp.zeros((w.shape[0], n), w.dtype)
    parts = [col(0), col(1), col(2), col(4), col(3), zeros(LANES - 2 * GLA_RANK),
             col(5), col(6), col(7),
             col(8), col(9), col(10), col(13), col(11), col(12), zeros(LANES - 4 * GDN_HEADS)]
    return jnp.concatenate(parts, axis=1).astype(BF16)


def _inproj_body(x_ref, mod_ref, w_ref, gla_ref, na_ref, gdn_ref):
    h = _ln_rows(x_ref[...]) * mod_ref[0, 1:2, :] + mod_ref[0, 0:1, :]
    hb = h.astype(BF16)
    gla_ref[...] = jnp.dot(hb, w_ref[:, 0:GLA_SEG_W], preferred_element_type=F32)
    na_ref[...] = jnp.dot(hb, w_ref[:, GLA_SEG_W:GLA_SEG_W + NA_SEG_W], preferred_element_type=F32).astype(BF16)
    gdn_ref[...] = jnp.dot(hb, w_ref[:, GLA_SEG_W + NA_SEG_W:], preferred_element_type=F32)


def _inproj(x, mod, w_bf, rows_per_group):
    n, d = x.shape
    n_groups = mod.shape[0]
    row = lambda i: (i, 0)
    return pl.pallas_call(
        _inproj_body,
        out_shape=(jax.ShapeDtypeStruct((n, GLA_SEG_W), F32), jax.ShapeDtypeStruct((n, NA_SEG_W), BF16),
                   jax.ShapeDtypeStruct((n, GDN_SEG_W), F32)),
        grid=(n // ROW_TILE,),
        in_specs=[
            pl.BlockSpec((ROW_TILE, d), row),
            pl.BlockSpec((1, 2, d), lambda i: (_group_of_block(i, rows_per_group, n_groups), 0, 0)),
            pl.BlockSpec((d, D_IN_SEG), lambda i: (0, 0)),
        ],
        out_specs=(pl.BlockSpec((ROW_TILE, GLA_SEG_W), row), pl.BlockSpec((ROW_TILE, NA_SEG_W), row),
                   pl.BlockSpec((ROW_TILE, GDN_SEG_W), row)),
        compiler_params=pltpu.CompilerParams(dimension_semantics=("arbitrary",),
                                             vmem_limit_bytes=VMEM_LIMIT_BYTES),
        name="inproj",
    )(x, mod, w_bf)


def _outproj_body(gf_ref, gb_ref, gg_ref, gn_ref, na_ref, df_ref, db_ref, dz_ref, dn_ref, x_ref, mod_ref, ln_ref,
                  w_ref, wr_ref, br_ref, xo_ref, h_ref, lg_ref):
    o = gf_ref[...] + gb_ref[...]
    sq = o * o
    sq_hi = sq.astype(BF16)
    sq_lo = (sq - sq_hi.astype(F32)).astype(BF16)
    head_mean = ((lax.broadcasted_iota(jnp.int32, (GLA_V_W, GLA_V_W), 0) // GLA_DV)
                 == (lax.broadcasted_iota(jnp.int32, (GLA_V_W, GLA_V_W), 1) // GLA_DV)
                 ).astype(BF16) * (1.0 / GLA_DV)
    ms = (jnp.dot(sq_hi, head_mean, preferred_element_type=F32)
          + jnp.dot(sq_lo, head_mean, preferred_element_type=F32))
    gla = o * lax.rsqrt(ms + RMS_EPS) * gn_ref[...] * jax.nn.silu(gg_ref[...])
    od = df_ref[...] + db_ref[...]
    parts = []
    for hd in range(GDN_HEADS):
        seg = od[:, hd * GDN_DV:(hd + 1) * GDN_DV]
        parts.append(seg * lax.rsqrt(jnp.mean(seg * seg, axis=-1, keepdims=True) + RMS_EPS))
    gdn = jnp.concatenate(parts, 1) * dn_ref[...] * jax.nn.silu(dz_ref[...])
    a = (jnp.dot(gla.astype(BF16), w_ref[0:GLA_V_W, :], preferred_element_type=F32)
         + jnp.dot(na_ref[...], w_ref[GLA_V_W:GLA_V_W + NA_W, :], preferred_element_type=F32)
         + jnp.dot(gdn.astype(BF16), w_ref[GLA_V_W + NA_W:, :], preferred_element_type=F32))
    y = DN_ALPHA * x_ref[...] + mod_ref[0, 0:1, :] * a
    xn = _ln_rows(y) * ln_ref[0:1, :] + ln_ref[1:2, :]
    xo_ref[...] = xn
    h = _ln_rows(xn) * mod_ref[0, 2:3, :] + mod_ref[0, 1:2, :]
    h_ref[...] = h
    lg_ref[...] = jnp.dot(h, wr_ref[...], preferred_element_type=F32, precision=_HI) + br_ref[...]


def _outproj(n, gla_f, gla_b, gla_cols, gla_norm, na, gdn_f, gdn_b, gdn_cols, gdn_norm, x, mod, ln, w_bf,
             w_router_pad, b_router_pad, rows_per_group):
    d = x.shape[1]
    n_groups = mod.shape[0]
    row = lambda i: (i, 0)
    const = lambda i: (0, 0)
    return pl.pallas_call(
        _outproj_body,
        out_shape=(jax.ShapeDtypeStruct((n, d), F32), jax.ShapeDtypeStruct((n, d), F32),
                   jax.ShapeDtypeStruct((n, LANES), F32)),
        grid=(n // ROW_TILE,),
        in_specs=[
            pl.BlockSpec((ROW_TILE, GLA_V_W), row),
            pl.BlockSpec((ROW_TILE, GLA_V_W), row),
            pl.BlockSpec((ROW_TILE, GLA_V_W), lambda i: (i, (2 * GLA_QK_W + GLA_V_W) // GLA_V_W)),
            pl.BlockSpec((1, GLA_V_W), const),
            pl.BlockSpec((ROW_TILE, NA_W), row),
            pl.BlockSpec((ROW_TILE, GDN_V_W), row),
            pl.BlockSpec((ROW_TILE, GDN_V_W), row),
            pl.BlockSpec((ROW_TILE, GDN_V_W), lambda i: (i, (2 * GDN_QK_W + GDN_V_W) // GDN_V_W)),
            pl.BlockSpec((1, GDN_V_W), const),
            pl.BlockSpec((ROW_TILE, d), row),
            pl.BlockSpec((1, 3, d), lambda i: (_group_of_block(i, rows_per_group, n_groups), 0, 0)),
            pl.BlockSpec((2, d), const),
            pl.BlockSpec((MIX_W, d), const),
            pl.BlockSpec((d, LANES), const),
            pl.BlockSpec((1, LANES), const),
        ],
        out_specs=(pl.BlockSpec((ROW_TILE, d), row), pl.BlockSpec((ROW_TILE, d), row),
                   pl.BlockSpec((ROW_TILE, LANES), row)),
        compiler_params=pltpu.CompilerParams(dimension_semantics=("arbitrary",),
                                             vmem_limit_bytes=VMEM_LIMIT_BYTES),
        name="outproj",
    )(gla_f, gla_b, gla_cols, gla_norm, na, gdn_f, gdn_b, gdn_cols, gdn_norm, x, mod, ln, w_bf, w_router_pad,
      b_router_pad)


def _moe_body(be_ref, nv_ref, x_ref, wgu_ref, bgu_ref, wd_ref, bd_ref, o_ref, wgu_bf, wd_bf):
    i = pl.program_id(0)
    e = be_ref[i]
    e_prev = be_ref[jnp.maximum(i - 1, 0)]
    valid = i < nv_ref[0]

    @pl.when(valid & ((i == 0) | (e != e_prev)))
    def _():
        wgu_bf[...] = wgu_ref[0, 0].astype(BF16)
        wd_bf[...] = wd_ref[0, 0].astype(BF16)

    @pl.when(valid)
    def _():
        x = x_ref[...].astype(BF16)
        acc = jnp.zeros((MOE_BM, D_MODEL), F32) + bd_ref[0, 0]
        for c in range(D_EXPERT // F_CHUNK):
            lo = c * F_CHUNK
            gt = jnp.dot(x, wgu_bf[:, lo:lo + F_CHUNK], preferred_element_type=F32)
            gt = gt + bgu_ref[0, 0, :, lo:lo + F_CHUNK]
            up = jnp.dot(x, wgu_bf[:, D_EXPERT + lo:D_EXPERT + lo + F_CHUNK], preferred_element_type=F32)
            up = up + bgu_ref[0, 0, :, D_EXPERT + lo:D_EXPERT + lo + F_CHUNK]
            gt = jnp.minimum(gt, SWIGLU_LIMIT)
            up = jnp.clip(up, -SWIGLU_LIMIT, SWIGLU_LIMIT)
            act = (up + 1.0) * gt * jax.nn.sigmoid(SWIGLU_ALPHA * gt)
            acc = acc + jnp.dot(act.astype(BF16), wd_bf[lo:lo + F_CHUNK, :], preferred_element_type=F32)
        o_ref[...] = acc

    @pl.when(jnp.logical_not(valid))
    def _():
        o_ref[...] = jnp.zeros_like(o_ref)


def _moe_experts(layer, block_e, n_valid, xs, w_gate_up, b_gate_up, w_down, b_down):
    n_rows, d = xs.shape
    n_blocks = n_rows // MOE_BM
    f2 = 2 * D_EXPERT

    def blk(i, be, nv):
        return jnp.minimum(i, nv[0] - 1)

    grid_spec = pltpu.PrefetchScalarGridSpec(
        num_scalar_prefetch=2,
        grid=(n_blocks,),
        in_specs=[
            pl.BlockSpec((MOE_BM, d), lambda i, be, nv: (blk(i, be, nv), 0)),
            pl.BlockSpec((1, 1, d, f2), lambda i, be, nv: (layer, be[blk(i, be, nv)], 0, 0)),
            pl.BlockSpec((1, 1, 1, f2), lambda i, be, nv: (layer, be[blk(i, be, nv)], 0, 0)),
            pl.BlockSpec((1, 1, D_EXPERT, d), lambda i, be, nv: (layer, be[blk(i, be, nv)], 0, 0)),
            pl.BlockSpec((1, 1, 1, d), lambda i, be, nv: (layer, be[blk(i, be, nv)], 0, 0)),
        ],
        out_specs=pl.BlockSpec((MOE_BM, d), lambda i, be, nv: (i, 0)),
        scratch_shapes=[pltpu.VMEM((d, f2), BF16), pltpu.VMEM((D_EXPERT, d), BF16)],
    )
    return pl.pallas_call(
        _moe_body,
        out_shape=jax.ShapeDtypeStruct((n_rows, d), F32),
        grid_spec=grid_spec,
        compiler_params=pltpu.CompilerParams(dimension_semantics=("arbitrary",),
                                             vmem_limit_bytes=VMEM_LIMIT_BYTES),
        name="moe_experts",
    )(block_e, n_valid, xs, w_gate_up, b_gate_up.reshape(DEPTH, N_EXPERTS, 1, f2), w_down,
      b_down.reshape(DEPTH, N_EXPERTS, 1, d))


GATHER_ROWS = 2048


def _gather_body(group, nv_ref, idx_ref, src_ref, o_ref, sem):
    d = src_ref.shape[1]
    active = pl.program_id(0) * GATHER_ROWS < nv_ref[0]

    def row_copy(src_row, r):
        dst = o_ref.at[pl.ds(r // group, 1), pl.ds((r % group) * d, d)]
        return pltpu.make_async_copy(src_ref.at[pl.ds(src_row, 1)], dst, sem.at[0])

    @pl.when(active)
    def _():
        for r in range(GATHER_ROWS):
            row_copy(idx_ref[0, 0, r], r).start(priority=r % 2)
        for r in range(GATHER_ROWS):
            row_copy(0, r).wait()

    @pl.when(jnp.logical_not(active))
    def _():
        o_ref[...] = jnp.zeros_like(o_ref)


def _gather_rows(src, idx, n_valid, group=1):
    n_out = idx.shape[0]
    n_steps = n_out // GATHER_ROWS
    d = src.shape[1]
    grid_spec = pltpu.PrefetchScalarGridSpec(
        num_scalar_prefetch=1,
        grid=(n_steps,),
        in_specs=[
            pl.BlockSpec((1, 1, GATHER_ROWS), lambda i, nv: (i, 0, 0), memory_space=pltpu.SMEM),
            pl.BlockSpec(memory_space=pl.ANY),
        ],
        out_specs=pl.BlockSpec((GATHER_ROWS // group, group * d), lambda i, nv: (i, 0)),
        scratch_shapes=[pltpu.SemaphoreType.DMA((1,))],
    )
    return pl.pallas_call(
        functools.partial(_gather_body, group),
        out_shape=jax.ShapeDtypeStruct((n_out // group, group * d), src.dtype),
        grid_spec=grid_spec,
        compiler_params=pltpu.CompilerParams(dimension_semantics=("arbitrary",)),
        name="gather_rows",
    )(n_valid, idx.reshape(n_steps, 1, GATHER_ROWS), src)


def _combine_body(y_ref, g_ref, x_ref, mod_ref, ln_ref, o_ref):
    d = x_ref.shape[1]
    g = g_ref[...]
    f = g[:, 0:1] * y_ref[:, 0:d]
    for k in range(1, TOP_K):
        f = f + g[:, k:k + 1] * y_ref[:, k * d:(k + 1) * d]
    y = DN_ALPHA * x_ref[...] + mod_ref[0, 0:1, :] * f
    o_ref[...] = _ln_rows(y) * ln_ref[0:1, :] + ln_ref[1:2, :]


def _combine(y4, gate, x1, gate2, ln, rows_per_group):
    n, d = x1.shape
    n_groups = gate2.shape[0]
    row = lambda i: (i, 0)
    return pl.pallas_call(
        _combine_body,
        out_shape=jax.ShapeDtypeStruct((n, d), F32),
        grid=(n // ROW_TILE,),
        in_specs=[
            pl.BlockSpec((ROW_TILE, TOP_K * d), row),
            pl.BlockSpec((ROW_TILE, TOP_K), row),
            pl.BlockSpec((ROW_TILE, d), row),
            pl.BlockSpec((1, 1, d), lambda i: (_group_of_block(i, rows_per_group, n_groups), 0, 0)),
            pl.BlockSpec((2, d), lambda i: (0, 0)),
        ],
        out_specs=pl.BlockSpec((ROW_TILE, d), row),
        compiler_params=pltpu.CompilerParams(dimension_semantics=("arbitrary",),
                                             vmem_limit_bytes=VMEM_LIMIT_BYTES),
        name="combine",
    )(y4, gate, x1, gate2, ln)


def _moe(layer, h, logits, w_gate_up, b_gate_up, w_down, b_down):
    n_tok, d = h.shape
    top_logit, top_e = lax.top_k(logits, TOP_K)
    gate = jax.nn.softmax(top_logit, axis=-1)
    nk = n_tok * TOP_K
    flat_e = top_e.reshape(-1).astype(jnp.int32)
    onehot = (flat_e[:, None] == jnp.arange(N_EXPERTS, dtype=jnp.int32)[None, :]).astype(jnp.int32)
    csum = jnp.cumsum(onehot, axis=0)
    counts = csum[-1]
    rank = jnp.sum((csum - 1) * onehot, axis=1)
    padded = (counts + MOE_BM - 1) // MOE_BM * MOE_BM
    pad_end = jnp.cumsum(padded)
    pad_start = pad_end - padded
    dest = jnp.sum(onehot * pad_start[None, :], axis=1) + rank
    n_blocks = nk // MOE_BM + N_EXPERTS
    flat_tok = jnp.arange(nk, dtype=jnp.int32) // TOP_K
    buf_tok = jnp.zeros((n_blocks * MOE_BM,), jnp.int32).at[dest].set(flat_tok)
    block_start = jnp.arange(n_blocks, dtype=jnp.int32) * MOE_BM
    block_e = jnp.minimum(jnp.searchsorted(pad_end, block_start, side='right'), N_EXPERTS - 1).astype(jnp.int32)
    n_valid = (pad_end[-1] // MOE_BM).astype(jnp.int32).reshape(1)
    xs = _gather_rows(h, buf_tok, pad_end[-1].astype(jnp.int32).reshape(1))
    yb = _moe_experts(layer, block_e, n_valid, xs, w_gate_up, b_gate_up, w_down, b_down)
    return _gather_rows(yb, dest, jnp.full((1,), nk, jnp.int32), group=TOP_K), gate


NA_ROWS_PER_STEP = 4
NA_PATTERN_ROWS = (0, 1, 2, 3, GRID_W // 2, GRID_W - 3, GRID_W - 2, GRID_W - 1)


def _na_bias_table(rpb):
    rows = GRID_W
    pat_r = np.array(NA_PATTERN_ROWS)
    r0 = np.clip(pat_r - NA_WIN_R // 2, 0, rows - NA_WIN_R)
    dr = r0[:, None] + np.arange(NA_WIN_R)[None, :] - pat_r[:, None]
    cidx = np.arange(GRID_W)
    c0 = np.clip(cidx - NA_WIN_C // 2, 0, GRID_W - NA_WIN_C)
    col_ok = (cidx[None, :] >= c0[:, None]) & (cidx[None, :] < c0[:, None] + NA_WIN_C)
    dc = np.clip(cidx[None, :] - cidx[:, None] + NA_WIN_C - 1, 0, 2 * NA_WIN_C - 2)
    n_dc = 2 * NA_WIN_C - 1
    sel = jnp.stack([jnp.stack([rpb[:, int(dr[p, i]) + NA_WIN_R - 1, :] for i in range(NA_WIN_R)], 1)
                     for p in range(len(NA_PATTERN_ROWS))], 0)
    onehot = jnp.asarray((dc[:, :, None] == np.arange(n_dc)[None, None, :]).astype(np.float32))
    bias = jnp.einsum('phic,qkc->phqik', sel, onehot, precision=_HI)
    bias = jnp.where(col_ok[None, None, :, None, :], bias, -jnp.inf)
    return bias.reshape(len(NA_PATTERN_ROWS), NA_HEADS * GRID_W, NA_WIN_R * GRID_W)


def _head_block_mask(n_q):
    shape = (NA_HEADS * n_q, NA_W)
    return (lax.broadcasted_iota(jnp.int32, shape, 0) // n_q) == (lax.broadcasted_iota(jnp.int32, shape, 1) // NA_HD)


def _heads_on_rows(q, mask):
    qt = jnp.concatenate([q] * NA_HEADS, axis=0)
    return jnp.where(mask, qt, jnp.zeros_like(qt)) * (NA_HD ** -0.5)


def _heads_to_lanes(o_all, n_q, n_heads, head_w):
    lane_h = lax.broadcasted_iota(jnp.int32, (n_q, n_heads * head_w), 1) // head_w
    out = jnp.zeros((n_q, n_heads * head_w), F32)
    for h in range(n_heads):
        out = out + jnp.where(lane_h == h, o_all[h * n_q:(h + 1) * n_q], 0.0)
    return out


def _na_body(q_ref, k_ref, v_ref, kc_ref, vc_ref, bias_ref, o_ref):
    g = pl.program_id(1)
    mask = _head_block_mask(GRID_W)
    kc = kc_ref[...]
    vc = vc_ref[...]
    rows = GRID_W
    for i in range(NA_ROWS_PER_STEP):
        r = g * NA_ROWS_PER_STEP + i
        r0 = jnp.clip(r - NA_WIN_R // 2, 0, rows - NA_WIN_R)
        pat = jnp.where(r < 4, r, jnp.where(r > rows - 4, r - (rows - 8), 4))
        qb = _heads_on_rows(q_ref[i * GRID_W:(i + 1) * GRID_W, :], mask)
        start = pl.multiple_of(r0 * GRID_W, GRID_W)
        ks = k_ref[pl.ds(start, NA_WIN_R * GRID_W), :]
        vs = v_ref[pl.ds(start, NA_WIN_R * GRID_W), :]
        s_loc = lax.dot_general(qb, ks, _NT, preferred_element_type=F32) + bias_ref[pat]
        s_ctx = lax.dot_general(qb, kc, _NT, preferred_element_type=F32)
        m = jnp.maximum(jnp.max(s_loc, axis=-1, keepdims=True), jnp.max(s_ctx, axis=-1, keepdims=True))
        p_loc = jnp.exp(s_loc - m)
        p_ctx = jnp.exp(s_ctx - m)
        denom = jnp.sum(p_loc, axis=-1, keepdims=True) + jnp.sum(p_ctx, axis=-1, keepdims=True)
        o_all = (jnp.dot(p_loc.astype(BF16), vs, preferred_element_type=F32)
                 + jnp.dot(p_ctx.astype(BF16), vc, preferred_element_type=F32)) / denom
        o_ref[i * GRID_W:(i + 1) * GRID_W, :] = _heads_to_lanes(o_all, GRID_W, NA_HEADS, NA_HD).astype(o_ref.dtype)


def _na_latent(qkv, bias, b_, S, L):
    tq = NA_ROWS_PER_STEP * GRID_W
    steps = S // tq
    ctx_blk0 = b_ * S // L
    return pl.pallas_call(
        _na_body,
        out_shape=jax.ShapeDtypeStruct((b_ * S, NA_W), BF16),
        grid=(b_, steps),
        in_specs=[
            pl.BlockSpec((tq, NA_W), lambda b, g: (b * steps + g, 0)),
            pl.BlockSpec((S, NA_W), lambda b, g: (b, 1)),
            pl.BlockSpec((S, NA_W), lambda b, g: (b, 2)),
            pl.BlockSpec((L, NA_W), lambda b, g: (ctx_blk0 + b, 1)),
            pl.BlockSpec((L, NA_W), lambda b, g: (ctx_blk0 + b, 2)),
            pl.BlockSpec(bias.shape, lambda b, g: (0, 0, 0)),
        ],
        out_specs=pl.BlockSpec((tq, NA_W), lambda b, g: (b * steps + g, 0)),
        compiler_params=pltpu.CompilerParams(dimension_semantics=("arbitrary", "arbitrary"),
                                             vmem_limit_bytes=VMEM_LIMIT_BYTES),
        name="na_latent",
    )(qkv, qkv, qkv, qkv, qkv, bias)


def _na_ctx_body(q_ref, k_ref, v_ref, o_ref):
    n_q = q_ref.shape[0]
    qb = _heads_on_rows(q_ref[...], _head_block_mask(n_q))
    s = lax.dot_general(qb, k_ref[...], _NT, preferred_element_type=F32)
    p = jnp.exp(s - jnp.max(s, axis=-1, keepdims=True))
    o_all = jnp.dot(p.astype(BF16), v_ref[...], preferred_element_type=F32) / jnp.sum(p, axis=-1, keepdims=True)
    o_ref[...] = _heads_to_lanes(o_all, n_q, NA_HEADS, NA_HD).astype(o_ref.dtype)


def _na_context(qkv, b_, S, L):
    ctx_blk0 = b_ * S // L
    return pl.pallas_call(
        _na_ctx_body,
        out_shape=jax.ShapeDtypeStruct((b_ * L, NA_W), BF16),
        grid=(b_,),
        in_specs=[pl.BlockSpec((L, NA_W), lambda b: (ctx_blk0 + b, 0)),
                  pl.BlockSpec((L, NA_W), lambda b: (ctx_blk0 + b, 1)),
                  pl.BlockSpec((L, NA_W), lambda b: (ctx_blk0 + b, 2))],
        out_specs=pl.BlockSpec((L, NA_W), lambda b: (b, 0)),
        compiler_params=pltpu.CompilerParams(dimension_semantics=("arbitrary",)),
        name="na_context",
    )(qkv, qkv, qkv)


GLA_BLOCK = 256
GLA_SUB = 8


def _rope_lane_tables(S, L):
    half = GLA_DK // 2
    inv_freq = ROPE_BASE ** (-jnp.arange(0, half, 2, dtype=F32) / half)
    t = jnp.arange(S)
    d = np.arange(GLA_QK_W) % GLA_DK
    use_col = d >= half
    fidx = d % (half // 2)
    sign = np.where((d % half) < half // 2, -1.0, 1.0).astype(np.float32)
    pos = jnp.where(use_col[None, :], (t % GRID_W)[:, None], (t // GRID_W)[:, None]).astype(F32)
    ang = pos * inv_freq[fidx][None, :]
    cos = jnp.concatenate([jnp.cos(ang), jnp.ones((L, GLA_QK_W), F32)], 0)
    sin = jnp.concatenate([jnp.sin(ang) * sign[None, :], jnp.zeros((L, GLA_QK_W), F32)], 0)
    return cos, sin


def _rope_lanes(x, cos, sin):
    pair = GLA_DK // 4
    lane = lax.broadcasted_iota(jnp.int32, x.shape, 1)
    first = (lane % (2 * pair)) < pair
    partner = jnp.where(first, pltpu.roll(x, LANES - pair, 1), pltpu.roll(x, pair, 1))
    return x * cos + partner * sin


def _bcast_rows(x, r, n):
    return jnp.broadcast_to(x[r:r + 1, :], (n, x.shape[1]))


def _gla_consts(rev):
    C = GLA_CHUNK
    r = lax.broadcasted_iota(jnp.int32, (C, C), 0)
    c = lax.broadcasted_iota(jnp.int32, (C, C), 1)
    tri = jnp.where((c >= r) if rev else (c <= r), 1.0, 0.0).astype(F32)
    n_exp = (C // GLA_SUB - 1) * GLA_QK_W
    head_mask_q = ((lax.broadcasted_iota(jnp.int32, (GLA_HEADS * C, n_exp), 0) // C)
                   == ((lax.broadcasted_iota(jnp.int32, (GLA_HEADS * C, n_exp), 1) % GLA_QK_W) // GLA_DK))
    s3 = ((lax.broadcasted_iota(jnp.int32, (GLA_QK_W, GLA_V_W), 0) // GLA_DK)
          == (lax.broadcasted_iota(jnp.int32, (GLA_QK_W, GLA_V_W), 1) // GLA_DV)).astype(BF16)
    n_p = C * GLA_SUB
    rsum = (lax.broadcasted_iota(jnp.int32, (C, n_p), 0)
            == lax.broadcasted_iota(jnp.int32, (C, n_p), 1) // GLA_SUB).astype(BF16)
    row = lax.broadcasted_iota(jnp.int32, (n_p, GLA_QK_W), 0)
    t_loc, s_loc = (row // GLA_SUB) % GLA_SUB, row % GLA_SUB
    diag_ok = (s_loc >= t_loc) if rev else (s_loc <= t_loc)
    st_mask = ((lax.broadcasted_iota(jnp.int32, (GLA_V_W, GLA_QK_W), 0) // GLA_DV)
               == (lax.broadcasted_iota(jnp.int32, (GLA_V_W, GLA_QK_W), 1) // GLA_DK))
    return tri, head_mask_q, s3, rsum, diag_ok, st_mask


def _gla_chunk(q, k, v, la, st, rev, consts):
    tri, head_mask_q, s3, rsum, diag_ok, st_mask = consts
    C, nb = GLA_CHUNK, GLA_CHUNK // GLA_SUB
    gc = jnp.dot(tri, la, precision=_HI, preferred_element_type=F32)
    zeros = jnp.zeros((GLA_SUB, GLA_QK_W), F32)
    if not rev:
        g_tot = gc[C - 1:C]
        ref_rows = [None] + [GLA_SUB * i - 1 for i in range(1, nb)]
        blk_rows = [GLA_SUB * j + GLA_SUB - 1 for j in range(nb)]
    else:
        g_tot = gc[0:1]
        ref_rows = [GLA_SUB * (i + 1) for i in range(nb - 1)] + [None]
        blk_rows = [GLA_SUB * j for j in range(nb)]
    g_ref = jnp.concatenate([zeros if r is None else _bcast_rows(gc, r, GLA_SUB) for r in ref_rows], 0)
    g_blk = jnp.concatenate([_bcast_rows(gc, r, GLA_SUB) for r in blk_rows], 0)
    qt = q * jnp.exp(gc - g_ref)
    kh = k * jnp.exp(g_blk - gc)
    sub = lax.broadcasted_iota(jnp.int32, (C, GLA_QK_W), 0) // GLA_SUB
    k_slabs, q_slabs = [], []
    for i in range(nb):
        if ref_rows[i] is None:
            continue
        keys_ok = (sub > i) if rev else (sub < i)
        between = jnp.minimum(gc[ref_rows[i]:ref_rows[i] + 1] - g_blk, 0.0)
        k_slabs.append(jnp.where(keys_ok, kh * jnp.exp(between), 0.0))
        q_slabs.append(jnp.where(sub == i, qt, 0.0))
    k_exp = jnp.concatenate(k_slabs, 1).astype(BF16)
    q_exp = jnp.concatenate(q_slabs, 1)
    q_exp = jnp.where(head_mask_q, jnp.concatenate([q_exp] * GLA_HEADS, 0), 0.0).astype(BF16)
    attn_off = lax.dot_general(q_exp, k_exp, _NT, preferred_element_type=F32)
    vb = v.astype(BF16)
    o_off_all = jnp.dot(attn_off.astype(BF16), vb, preferred_element_type=F32)
    o = _heads_to_lanes(o_off_all, C, GLA_HEADS, GLA_DV)
    pieces = []
    for i in range(nb):
        lo = i * GLA_SUB
        k_i, g_i = k[lo:lo + GLA_SUB], gc[lo:lo + GLA_SUB]
        for t in range(GLA_SUB):
            e = jnp.exp(jnp.minimum(_bcast_rows(gc, lo + t, GLA_SUB) - g_i, 0.0))
            pieces.append(_bcast_rows(q, lo + t, GLA_SUB) * k_i * e)
    p = jnp.where(diag_ok, jnp.concatenate(pieces, 0), 0.0).astype(BF16)
    a2 = jnp.dot(p, s3, preferred_element_type=F32)
    v_rep = jnp.concatenate([v[i * GLA_SUB:(i + 1) * GLA_SUB] for i in range(nb) for _ in range(GLA_SUB)], 0)
    o = o + jnp.dot(rsum, (a2 * v_rep).astype(BF16), preferred_element_type=F32)
    q_dec = (q * jnp.exp(gc)).astype(BF16)
    k_dec = (k * jnp.exp(g_tot - gc)).astype(BF16)
    o = o + lax.dot_general(q_dec, st.astype(BF16), _NT, preferred_element_type=F32)
    upd = lax.dot_general(vb, k_dec, _TN, preferred_element_type=F32)
    return o, st * jnp.exp(g_tot) + jnp.where(st_mask, upd, 0.0)


def _gla_body(xf_ref, xb_ref, cf_ref, sf_ref, cb_ref, sb_ref, w2_ref, b2_ref, of_ref, ob_ref,
              q_s, k_s, la_s, st_s):
    @pl.when(pl.program_id(1) == 0)
    def _():
        st_s[...] = jnp.zeros_like(st_s)

    gate_lo = 2 * GLA_QK_W + 2 * GLA_V_W
    for d, (x_ref, cos_ref, sin_ref) in enumerate(((xf_ref, cf_ref, sf_ref), (xb_ref, cb_ref, sb_ref))):
        z = jnp.dot(x_ref[:, gate_lo:], w2_ref[...], precision=_HI, preferred_element_type=F32) + b2_ref[...]
        la_s[d] = jax.nn.log_sigmoid(z[:, d * GLA_QK_W:(d + 1) * GLA_QK_W]) * (1.0 / GLA_TAU)
        q_s[d] = _rope_lanes(x_ref[:, 0:GLA_QK_W], cos_ref[...], sin_ref[...]) * (GLA_DK ** -0.5)
        k_s[d] = _rope_lanes(x_ref[:, GLA_QK_W:2 * GLA_QK_W], cos_ref[...], sin_ref[...])

    consts = (_gla_consts(False), _gla_consts(True))
    n_chunks = GLA_BLOCK // GLA_CHUNK

    def step(i, carry):
        starts = (i * GLA_CHUNK, (n_chunks - 1 - i) * GLA_CHUNK)
        for d, (x_ref, o_ref) in enumerate(((xf_ref, of_ref), (xb_ref, ob_ref))):
            rows = pl.ds(pl.multiple_of(starts[d], GLA_CHUNK), GLA_CHUNK)
            v = x_ref[rows, 2 * GLA_QK_W:2 * GLA_QK_W + GLA_V_W]
            o, st = _gla_chunk(q_s[d, rows, :], k_s[d, rows, :], v, la_s[d, rows, :], st_s[d], d == 1, consts[d])
            o_ref[rows, :] = o
            st_s[d] = st
        return carry

    lax.fori_loop(0, n_chunks, step, 0, unroll=2)


def _gla_gate_weights(w_a2, b_a):
    w2 = jnp.zeros((LANES, 2 * GLA_QK_W), F32)
    w2 = w2.at[0:GLA_RANK, 0:GLA_QK_W].set(w_a2[0]).at[GLA_RANK:2 * GLA_RANK, GLA_QK_W:].set(w_a2[1])
    return w2, b_a.reshape(1, 2 * GLA_QK_W)


def _gla_scan(cols, cos, sin, w2, b2, b_, S, L):
    n = cols.shape[0]
    lat_blocks = S // GLA_BLOCK
    ctx0 = b_ * S // GLA_BLOCK
    assert L == GLA_BLOCK
    fwd = lambda b, j: jnp.where(j == 0, ctx0 + b, b * lat_blocks + j - 1)
    bwd = lambda b, j: jnp.where(j == 0, ctx0 + b, b * lat_blocks + lat_blocks - j)
    fwd_t = lambda b, j: jnp.where(j == 0, lat_blocks, j - 1)
    bwd_t = lambda b, j: jnp.where(j == 0, lat_blocks, lat_blocks - j)
    blk = lambda f: (lambda b, j: (f(b, j), 0))
    const = lambda b, j: (0, 0)
    return pl.pallas_call(
        _gla_body,
        out_shape=(jax.ShapeDtypeStruct((n, GLA_V_W), F32), jax.ShapeDtypeStruct((n, GLA_V_W), F32)),
        grid=(b_, lat_blocks + 1),
        in_specs=[
            pl.BlockSpec((GLA_BLOCK, GLA_SEG_W), blk(fwd)),
            pl.BlockSpec((GLA_BLOCK, GLA_SEG_W), blk(bwd)),
            pl.BlockSpec((GLA_BLOCK, GLA_QK_W), blk(fwd_t)),
            pl.BlockSpec((GLA_BLOCK, GLA_QK_W), blk(fwd_t)),
            pl.BlockSpec((GLA_BLOCK, GLA_QK_W), blk(bwd_t)),
            pl.BlockSpec((GLA_BLOCK, GLA_QK_W), blk(bwd_t)),
            pl.BlockSpec((LANES, 2 * GLA_QK_W), const),
            pl.BlockSpec((1, 2 * GLA_QK_W), const),
        ],
        out_specs=(pl.BlockSpec((GLA_BLOCK, GLA_V_W), blk(fwd)), pl.BlockSpec((GLA_BLOCK, GLA_V_W), blk(bwd))),
        scratch_shapes=[pltpu.VMEM((2, GLA_BLOCK, GLA_QK_W), F32), pltpu.VMEM((2, GLA_BLOCK, GLA_QK_W), F32),
                        pltpu.VMEM((2, GLA_BLOCK, GLA_QK_W), F32), pltpu.VMEM((2, GLA_V_W, GLA_QK_W), F32)],
        compiler_params=pltpu.CompilerParams(dimension_semantics=("arbitrary", "arbitrary"),
                                             vmem_limit_bytes=VMEM_LIMIT_BYTES),
        name="gla_scan",
    )(cols, cols, cos, sin, cos, sin, w2, b2)


SUBLANES = 8
GDN_CONV_W = 2 * GDN_QK_W + GDN_V_W
GDN_GATE_LO = 2 * GDN_QK_W + 2 * GDN_V_W
GDN_BLOCK = 256
GDN_HT = GDN_HEADS * GDN_CHUNK


def _stack_heads(x):
    return jnp.concatenate([x[:, h * GDN_DK:(h + 1) * GDN_DK] for h in range(GDN_HEADS)], 0)


def _stack_cols(a, lane0, width):
    return jnp.concatenate([jnp.broadcast_to(a[:, lane0 + h:lane0 + h + 1], (a.shape[0], width))
                            for h in range(GDN_HEADS)], 0)


def _block_diag(xs, mask):
    return jnp.where(mask, jnp.concatenate([xs] * GDN_HEADS, 1), 0.0).astype(BF16)


def _gdn_consts(rev):
    C, HT = GDN_CHUNK, GDN_HT
    rr = lax.broadcasted_iota(jnp.int32, (HT, HT), 0)
    cc = lax.broadcasted_iota(jnp.int32, (HT, HT), 1)
    same_head = (rr // C) == (cc // C)
    t, s = rr % C, cc % C
    strict = same_head & ((s > t) if rev else (s < t))
    incl = same_head & ((s >= t) if rev else (s <= t))
    bd_mask = ((lax.broadcasted_iota(jnp.int32, (HT, GDN_QK_W), 0) // C)
               == (lax.broadcasted_iota(jnp.int32, (HT, GDN_QK_W), 1) // GDN_DK))
    return strict, incl, bd_mask


def _chunk_cumsum_matrix(n, chunk, rev):
    r = lax.broadcasted_iota(jnp.int32, (n, n), 0)
    c = lax.broadcasted_iota(jnp.int32, (n, n), 1)
    return jnp.where((r // chunk == c // chunk) & ((c >= r) if rev else (c <= r)), 1.0, 0.0).astype(F32)


def _gdn_chunk(qs, ks, vs, gc, g_lane0, bcol, st, rev, consts):
    strict, incl, bd_mask = consts
    C = GDN_CHUNK
    gcol = _stack_cols(gc, g_lane0, GDN_HT)
    g_t = gc.T
    g_row = jnp.concatenate([g_t[g_lane0 + h:g_lane0 + h + 1, :] for h in range(GDN_HEADS)], 1)
    dec = jnp.exp(jnp.minimum(gcol - g_row, 0.0))
    HT = GDN_HT
    kb = ks * bcol
    kq = jnp.concatenate([_block_diag(kb, bd_mask), _block_diag(qs, bd_mask)], 0)
    nq = lax.dot_general(kq, _block_diag(ks, bd_mask), _NT, preferred_element_type=F32)
    n = jnp.where(strict, nq[:HT] * dec, 0.0)
    qk = jnp.where(incl, nq[HT:] * dec, 0.0)
    tp = -n
    nb = n.astype(BF16)
    p = jnp.dot(nb, nb, preferred_element_type=F32)
    n_factors = int(np.log2(C)) - 1
    for it in range(n_factors):
        pb = p.astype(BF16)
        if it < n_factors - 1:
            both = jnp.dot(jnp.concatenate([tp.astype(BF16), pb], 0), pb, preferred_element_type=F32)
            tp, p = tp + p + both[:HT], both[HT:]
        else:
            tp = tp + p + jnp.dot(tp.astype(BF16), pb, preferred_element_type=F32)
    g = gcol[:, :GDN_DK]
    last = 0 if rev else C - 1
    g_last = jnp.concatenate([_bcast_rows(g, h * C + last, C) for h in range(GDN_HEADS)], 0)
    eg = jnp.exp(g)
    q_dec = qs * eg
    kbe = kb * eg
    k_dec = ks * jnp.exp(g_last - g)
    from_state = jnp.dot(jnp.concatenate([_block_diag(kbe, bd_mask), _block_diag(q_dec, bd_mask)], 0),
                         st.astype(BF16), preferred_element_type=F32)
    r = vs * bcol - from_state[:HT]
    v_new = r + jnp.dot(tp.astype(BF16), r.astype(BF16), preferred_element_type=F32)
    vnb = v_new.astype(BF16)
    o = from_state[HT:] + jnp.dot(qk.astype(BF16), vnb, preferred_element_type=F32)
    a_last = jnp.exp(jnp.concatenate([g_last[h * C:(h + 1) * C] for h in range(GDN_HEADS)
                                      for _ in range(GDN_DK // C)], 0))
    st_new = st * a_last + lax.dot_general(_block_diag(k_dec, bd_mask), vnb, _TN, preferred_element_type=F32)
    return o, st_new


def _gdn_prep(x_ref, prev_ref, next_ref, has_prev, has_next, w_ref, nega_ref, dt_ref):
    pad = GDN_CONV // 2
    cur = x_ref[:, 0:GDN_CONV_W]
    before = jnp.where(has_prev, prev_ref[SUBLANES - pad:SUBLANES, 0:GDN_CONV_W], 0.0)
    after = jnp.where(has_next, next_ref[0:pad, 0:GDN_CONV_W], 0.0)
    xe = jnp.concatenate([before, cur, after], 0)
    y = w_ref[0:1, :] * xe[0:GDN_BLOCK]
    for j in range(1, GDN_CONV):
        y = y + w_ref[j:j + 1, :] * xe[j:j + GDN_BLOCK]
    y = y * jax.nn.sigmoid(y)

    def l2n(t):
        parts = []
        for h in range(GDN_HEADS):
            seg = t[:, h * GDN_DK:(h + 1) * GDN_DK]
            parts.append(seg * lax.rsqrt(jnp.sum(seg * seg, -1, keepdims=True) + 1e-6))
        return jnp.concatenate(parts, 1)

    q = l2n(y[:, 0:GDN_QK_W]) * (GDN_DK ** -0.5)
    k = l2n(y[:, GDN_QK_W:2 * GDN_QK_W])
    v = y[:, 2 * GDN_QK_W:]
    tail = x_ref[:, GDN_GATE_LO:]
    beta = jax.nn.sigmoid(tail)
    tb = tail + dt_ref[...]
    softplus = jnp.maximum(tb, 0.0) + jnp.log(1.0 + jnp.exp(-jnp.abs(tb)))
    return q, k, v, beta, nega_ref[...] * softplus


def _gdn_body(lat_blocks, xf_ref, pf_ref, nf_ref, xb_ref, pb_ref, nb_ref, w_ref, nega_ref, dt_ref, of_ref, ob_ref,
              q_s, k_s, v_s, b_s, g_s, st_s):
    j = pl.program_id(1)

    @pl.when(j == 0)
    def _():
        st_s[...] = jnp.zeros_like(st_s)

    t_blk = (j - 1, lat_blocks - j)
    for d, (x_ref, p_ref, n_ref) in enumerate(((xf_ref, pf_ref, nf_ref), (xb_ref, pb_ref, nb_ref))):
        has_prev = (j > 0) & (t_blk[d] > 0)
        has_next = (j > 0) & (t_blk[d] < lat_blocks - 1)
        q_s[d], k_s[d], v_s[d], b_s[d], g = _gdn_prep(x_ref, p_ref, n_ref, has_prev, has_next,
                                                      w_ref, nega_ref, dt_ref)
        g_s[d] = jnp.dot(_chunk_cumsum_matrix(GDN_BLOCK, GDN_CHUNK, d == 1), g, precision=_HI,
                         preferred_element_type=F32)

    consts = (_gdn_consts(False), _gdn_consts(True))
    n_chunks = GDN_BLOCK // GDN_CHUNK

    def step(i, carry):
        starts = (i * GDN_CHUNK, (n_chunks - 1 - i) * GDN_CHUNK)
        for d, o_ref in enumerate((of_ref, ob_ref)):
            rows = pl.ds(pl.multiple_of(starts[d], GDN_CHUNK), GDN_CHUNK)
            bcol = _stack_cols(b_s[d, rows, :], d * GDN_HEADS, GDN_DK)
            o, st = _gdn_chunk(_stack_heads(q_s[d, rows, :]), _stack_heads(k_s[d, rows, :]),
                               _stack_heads(v_s[d, rows, :]), g_s[d, rows, :], (2 + d) * GDN_HEADS, bcol,
                               st_s[d], d == 1, consts[d])
            o_ref[rows, :] = jnp.concatenate([o[h * GDN_CHUNK:(h + 1) * GDN_CHUNK] for h in range(GDN_HEADS)], 1)
            st_s[d] = st
        return carry

    lax.fori_loop(0, n_chunks, step, 0)


def _gdn_gate_consts(a_log, dt_bias):
    nega = jnp.zeros((1, LANES), F32).at[0, 2 * GDN_HEADS:4 * GDN_HEADS].set(-jnp.exp(a_log).reshape(-1))
    dt = jnp.zeros((1, LANES), F32).at[0, 2 * GDN_HEADS:4 * GDN_HEADS].set(dt_bias.reshape(-1))
    return nega, dt


def _gdn_scan(cols, conv_w, nega, dt, b_, S, L):
    n = cols.shape[0]
    lat_blocks = S // GDN_BLOCK
    ctx0 = b_ * S // GDN_BLOCK
    assert L == GDN_BLOCK
    per8 = GDN_BLOCK // SUBLANES
    n8 = n // SUBLANES
    fwd = lambda b, j: jnp.where(j == 0, ctx0 + b, b * lat_blocks + j - 1)
    bwd = lambda b, j: jnp.where(j == 0, ctx0 + b, b * lat_blocks + lat_blocks - j)
    blk = lambda f: (lambda b, j: (f(b, j), 0))
    prev8 = lambda f: (lambda b, j: (jnp.maximum(f(b, j) * per8 - 1, 0), 0))
    next8 = lambda f: (lambda b, j: (jnp.minimum(f(b, j) * per8 + per8, n8 - 1), 0))
    const = lambda b, j: (0, 0)
    x_spec = lambda f: pl.BlockSpec((GDN_BLOCK, GDN_SEG_W), blk(f))
    halo = lambda m: pl.BlockSpec((SUBLANES, GDN_SEG_W), m)
    return pl.pallas_call(
        functools.partial(_gdn_body, lat_blocks),
        out_shape=(jax.ShapeDtypeStruct((n, GDN_V_W), F32), jax.ShapeDtypeStruct((n, GDN_V_W), F32)),
        grid=(b_, lat_blocks + 1),
        in_specs=[x_spec(fwd), halo(prev8(fwd)), halo(next8(fwd)), x_spec(bwd), halo(prev8(bwd)), halo(next8(bwd)),
                  pl.BlockSpec((GDN_CONV, GDN_CONV_W), const), pl.BlockSpec((1, LANES), const),
                  pl.BlockSpec((1, LANES), const)],
        out_specs=(pl.BlockSpec((GDN_BLOCK, GDN_V_W), blk(fwd)), pl.BlockSpec((GDN_BLOCK, GDN_V_W), blk(bwd))),
        scratch_shapes=[pltpu.VMEM((2, GDN_BLOCK, GDN_QK_W), F32), pltpu.VMEM((2, GDN_BLOCK, GDN_QK_W), F32),
                        pltpu.VMEM((2, GDN_BLOCK, GDN_V_W), F32), pltpu.VMEM((2, GDN_BLOCK, LANES), F32),
                        pltpu.VMEM((2, GDN_BLOCK, LANES), F32), pltpu.VMEM((2, GDN_QK_W, GDN_DV), F32)],
        compiler_params=pltpu.CompilerParams(dimension_semantics=("arbitrary", "arbitrary"),
                                             vmem_limit_bytes=VMEM_LIMIT_BYTES),
        name="gdn_scan",
    )(cols, cols, cols, cols, cols, cols, conv_w, nega, dt)


def kernel(x, c, ctx, c_ctx, w_ada, b_ada, w_in, gla_w_a2, gla_b_a, gla_norm, na_rpb, gdn_conv, gdn_a_log,
           gdn_dt_bias, gdn_norm, w_out, ln1_g, ln1_b, w_router, b_router, w_gate_up, b_gate_up, w_down, b_down,
           ln2_g, ln2_b):
    b_, S, D = x.shape
    L = ctx.shape[1]
    n_lat, n_ctx = b_ * S, b_ * L
    rope_cos, rope_sin = _rope_lane_tables(S, L)
    sc = jnp.concatenate([jax.nn.silu(c), jax.nn.silu(c_ctx)[None, :]], 0)
    xt = jnp.concatenate([x.reshape(n_lat, D), ctx.reshape(n_ctx, D)], 0)
    for l in range(DEPTH):
        last = l == DEPTH - 1
        m = (jnp.dot(sc, w_ada[l], precision=_HI) + b_ada[l]).reshape(b_ + 1, N_MOD, D)
        shift1, scale1, gate1, shift2, scale2, gate2 = (m[:, j] for j in range(N_MOD))
        gla_cols, na_cols, gdn_cols = _inproj(xt, jnp.stack([shift1, 1.0 + scale1], 1), _inproj_weights(w_in[l]), S)
        gla_w2, gla_b2 = _gla_gate_weights(gla_w_a2[l], gla_b_a[l])
        gla_f, gla_b = _gla_scan(gla_cols, rope_cos, rope_sin, gla_w2, gla_b2, b_, S, L)
        na = jnp.concatenate([_na_latent(na_cols, _na_bias_table(na_rpb[l]), b_, S, L),
                              _na_context(na_cols, b_, S, L)], 0)
        gdn_nega, gdn_dt = _gdn_gate_consts(gdn_a_log[l], gdn_dt_bias[l])
        gdn_f, gdn_b = _gdn_scan(gdn_cols, gdn_conv[l], gdn_nega, gdn_dt, b_, S, L)
        n_rows = n_lat if last else n_lat + n_ctx
        w_router_pad = jnp.pad(w_router[l], ((0, 0), (0, LANES - N_EXPERTS)))
        b_router_pad = jnp.pad(b_router[l], (0, LANES - N_EXPERTS)).reshape(1, LANES)
        x1, h2, logits = _outproj(n_rows, gla_f, gla_b, gla_cols, jnp.tile(gla_norm[l], GLA_HEADS)[None, :], na,
                                  gdn_f, gdn_b, gdn_cols, jnp.tile(gdn_norm[l], GDN_HEADS)[None, :], xt,
                                  jnp.stack([gate1, shift2, 1.0 + scale2], 1),
                                  jnp.stack([ln1_g[l], ln1_b[l]], 0), w_out[l].astype(BF16),
                                  w_router_pad, b_router_pad, S)
        y4, gate = _moe(l, h2, logits[:, :N_EXPERTS], w_gate_up, b_gate_up, w_down, b_down)
        xt = _combine(y4, gate, x1, gate2[:, None, :], jnp.stack([ln2_g[l], ln2_b[l]], 0), S)
    return xt[:n_lat].reshape(b_, S, D)
```
